```python
import math
import jax
import jax.numpy as jnp
from jax import lax
import numpy as np

D_MODEL = 4096
BATCH = 4
SEQ = 2048
DEPTH = 2
DEC_BATCH = 8
DEC_SEQ = 4
PAST_LEN = 16384
PAGE_SIZE = 128

HEAD_DIM = 128
MOBA_HEADS = 8
MOBA_BLOCK = 256
MOBA_TOPK = 3
DIFF_HEADS = 4
DIFF_VDIM = 2 * HEAD_DIM
NSA_HEADS = 8
NSA_GROUPS = 2
NSA_HPG = NSA_HEADS // NSA_GROUPS
NSA_CMP_BLOCK = 32
NSA_CMP_STRIDE = 16
NSA_CMP_HIDDEN = 2 * HEAD_DIM
NSA_SEL_BLOCK = 64
NSA_SEL_COUNT = 16
NSA_WINDOW = 512
NSA_WIN_QBLOCK = 128
REL_BUCKETS = 32
REL_MAX_DIST = 128
N_BIAS_HEADS = MOBA_HEADS + DIFF_HEADS + NSA_HEADS
N_BRANCH = 3
BRANCH_WIDTH = MOBA_HEADS * HEAD_DIM
D_FF = 11008
CONV_WIDTH = 3
DENSE_QBLOCK = 128
SPARSE_QBLOCK = 64
RMS_EPS = 1e-6
SUBLN_EPS = 1e-5
PROJ_SIZES = (
    MOBA_HEADS * HEAD_DIM, MOBA_HEADS * HEAD_DIM, MOBA_HEADS * HEAD_DIM,
    DIFF_HEADS * 2 * HEAD_DIM, DIFF_HEADS * 2 * HEAD_DIM, DIFF_HEADS * DIFF_VDIM,
    NSA_HEADS * HEAD_DIM,
    NSA_GROUPS * HEAD_DIM, NSA_GROUPS * HEAD_DIM,
    NSA_GROUPS * HEAD_DIM, NSA_GROUPS * HEAD_DIM,
    NSA_GROUPS * HEAD_DIM, NSA_GROUPS * HEAD_DIM,
    3 * NSA_HEADS,
    N_BRANCH * D_MODEL,
)
PROJ_WIDTH = sum(PROJ_SIZES)

kernel_name = 'hybrid_moba_diff_nsa_convffn_step'


def rmsnorm(x, g, eps=RMS_EPS):
    xf = x.astype(jnp.float32)
    y = xf * lax.rsqrt(jnp.mean(xf * xf, axis=-1, keepdims=True) + eps)
    return (y * g.astype(jnp.float32)).astype(x.dtype)


def masked_softmax(s, mask, axis):
    s = jnp.where(mask, s.astype(jnp.float32), -jnp.inf)
    m = jnp.max(s, axis=axis, keepdims=True)
    m = jnp.where(jnp.isfinite(m), m, 0.0)
    e = jnp.where(mask, jnp.exp(s - m), 0.0)
    return e / jnp.maximum(jnp.sum(e, axis=axis, keepdims=True), 1e-30)


def rel_bucket(dist):
    n = jnp.maximum(dist, 0)
    exact = REL_BUCKETS // 2
    nf = jnp.maximum(n, 1).astype(jnp.float32)
    large = exact + (jnp.log(nf / exact) / math.log(REL_MAX_DIST / exact)
                     * (REL_BUCKETS - exact)).astype(jnp.int32)
    return jnp.where(n < exact, n, jnp.minimum(large, REL_BUCKETS - 1))


def split_cols(a, sizes):
    out, start = [], 0
    for s in sizes:
        out.append(a[..., start:start + s])
        start += s
    return out


def gather_pages(pool, page_table):
    rows = pool[page_table]
    return rows.reshape((rows.shape[0], rows.shape[1] * rows.shape[2]) + rows.shape[3:])


def choose_blocks(score, own, n_top):
    own_b = jnp.broadcast_to(own[None, :, None, None].astype(jnp.int32), score.shape[:3] + (1,))
    if n_top == 0:
        return own_b, jnp.ones(own_b.shape, bool)
    top_s, top_i = lax.top_k(score, n_top)
    idx = jnp.concatenate([top_i.astype(jnp.int32), own_b], axis=-1)
    ok = jnp.concatenate([top_s > -jnp.inf, jnp.ones(own_b.shape, bool)], axis=-1)
    return idx, ok


def gathered_block_attention(q, q_pos, kb, vb, idx, ok, tab):
    B, T, G, J, D = q.shape
    blk = kb.shape[3]
    qc = min(SPARSE_QBLOCK, T)
    nc = T // qc
    scale = D ** -0.5
    tab_g = tab.reshape(REL_BUCKETS, G, J)
    g_ix = jnp.arange(G)

    def chunks(a):
        return a.reshape((B * nc, qc) + a.shape[2:])

    pos_c = jnp.broadcast_to(q_pos.reshape(1, nc, qc), (B, nc, qc)).reshape(B * nc, qc)
    b_c = jnp.repeat(jnp.arange(B), nc)

    def step(args):
        q_i, idx_i, ok_i, pos_i, b_i = args
        k_g = kb[b_i][g_ix[None, :, None], idx_i]
        v_g = vb[b_i][g_ix[None, :, None], idx_i]
        k_pos = idx_i[..., None] * blk + jnp.arange(blk)
        dist = pos_i[:, None, None, None] - k_pos
        mask = ok_i[..., None] & (dist >= 0)
        bias = tab_g[rel_bucket(dist), g_ix[None, :, None, None]]
        s = (jnp.einsum('qgjd,qgnkd->qgjnk', q_i, k_g).astype(jnp.float32) * scale
             + jnp.moveaxis(bias, -1, 2))
        p = masked_softmax(s, mask[:, :, None], axis=(3, 4))
        return jnp.einsum('qgjnk,qgnkd->qgjd', p.astype(v_g.dtype), v_g)

    o = lax.map(step, (chunks(q), chunks(idx), chunks(ok), pos_c, b_c))
    return o.reshape(B, T, G, J, D)


def moba_attention(q, q_start, k, v, tab):
    B, T, H, D = q.shape
    L = k.shape[1]
    q_pos = q_start + jnp.arange(T)
    nb = -(-L // MOBA_BLOCK)
    pad = nb * MOBA_BLOCK - L

    def blocks(a):
        a = jnp.pad(a, ((0, 0), (0, pad), (0, 0), (0, 0)))
        return a.reshape(B, nb, MOBA_BLOCK, H, D).transpose(0, 3, 1, 2, 4)

    kb, vb = blocks(k), blocks(v)
    k_mean = jnp.mean(kb, axis=3, dtype=jnp.float32)
    own = q_pos // MOBA_BLOCK
    elig = jnp.arange(nb)[None, :] < own[:, None]
    score = jnp.einsum('bthd,bhnd->bthn', q.astype(jnp.float32), k_mean)
    score = jnp.where(elig[None, :, None, :], score, -jnp.inf)
    n_top = min(MOBA_TOPK, (q_start + T - 1) // MOBA_BLOCK)
    idx, ok = choose_blocks(score, own, n_top)
    o = gathered_block_attention(q[:, :, :, None], q_pos, kb, vb, idx, ok, tab)
    return o[:, :, :, 0]


def diff_attention(q, q_start, k, v, lam, tab):
    B, T, H, _, D = q.shape
    L = k.shape[1]
    qb = min(DENSE_QBLOCK, T)
    nb = T // qb
    scale = D ** -0.5
    k_pos = jnp.arange(L)
    q_blocks = jnp.moveaxis(q.reshape(B, nb, qb, H, 2, D), 1, 0)
    q_pos = (q_start + jnp.arange(T)).reshape(nb, qb)

    def step(args):
        q_i, pos = args
        dist = pos[:, None] - k_pos[None, :]
        bias = jnp.moveaxis(tab[rel_bucket(dist)], -1, 0)
        s = (jnp.einsum('bqhmd,bkhmd->bhmqk', q_i, k).astype(jnp.float32) * scale
             + bias[None, :, None])
        p = masked_softmax(s, dist >= 0, axis=-1)
        a = p[:, :, 0] - lam * p[:, :, 1]
        return jnp.einsum('bhqk,bkhe->bqhe', a.astype(v.dtype), v)

    o = lax.map(step, (q_blocks, q_pos))
    return jnp.moveaxis(o, 0, 1).reshape(B, T, H, 2 * D)


def nsa_compress(k, pos_emb, w1, b1, w2, b2):
    B, L, G, D = k.shape
    r = NSA_CMP_BLOCK // NSA_CMP_STRIDE
    n_ch = L // NSA_CMP_STRIDE
    n_cmp = n_ch - r + 1
    ch = k[:, :n_ch * NSA_CMP_STRIDE].reshape(B, n_ch, NSA_CMP_STRIDE, G, D)
    pe = pos_emb.reshape(r, NSA_CMP_STRIDE, D)
    w1r = w1.reshape(r, NSA_CMP_STRIDE, D, NSA_CMP_HIDDEN)
    pre = b1
    for s in range(r):
        pre = pre + jnp.einsum('bnlgd,ldh->bngh', ch[:, s:s + n_cmp] + pe[s][:, None, :], w1r[s])
    return jnp.einsum('bngh,hd->bngd', jax.nn.gelu(pre), w2) + b2


def nsa_compressed_selected(q, q_start, kc, vc, ks, vs, cmp_pos, cmp_w1, cmp_b1, cmp_w2, cmp_b2, tab):
    B, T, G, J, D = q.shape
    L = kc.shape[1]
    scale = D ** -0.5
    q_pos = q_start + jnp.arange(T)
    k_cmp = nsa_compress(kc, cmp_pos[0], cmp_w1[0], cmp_b1[0], cmp_w2[0], cmp_b2[0])
    v_cmp = nsa_compress(vc, cmp_pos[1], cmp_w1[1], cmp_b1[1], cmp_w2[1], cmp_b2[1])
    n_cmp = k_cmp.shape[1]
    cmp_end = jnp.arange(n_cmp) * NSA_CMP_STRIDE + NSA_CMP_BLOCK - 1
    valid = cmp_end[None, :] <= q_pos[:, None]
    s = jnp.einsum('btgjd,bngd->bgjtn', q, k_cmp).astype(jnp.float32) * scale
    p = masked_softmax(s, valid, axis=-1)
    o_cmp = jnp.einsum('bgjtn,bngd->btgjd', p.astype(v_cmp.dtype), v_cmp)
    imp = jnp.sum(p, axis=2)
    nbs = -(-L // NSA_SEL_BLOCK)
    ratio = NSA_SEL_BLOCK // NSA_CMP_STRIDE
    front = NSA_CMP_BLOCK // NSA_CMP_STRIDE - 1
    span = ratio + front
    back = max(ratio * (nbs - 1) + span - front - n_cmp, 0)
    imp = jnp.pad(imp, ((0, 0), (0, 0), (0, 0), (front, back)))
    slc = sum(imp[..., o:o + ratio * nbs:ratio] for o in range(span))
    slc = jnp.transpose(slc, (0, 2, 1, 3))
    own = q_pos // NSA_SEL_BLOCK
    blk = jnp.arange(nbs)
    elig = blk[None, :] < own[:, None]
    forced = (blk[None, :] == 0) | (blk[None, :] == own[:, None] - 1)
    score = jnp.where(forced[None, :, None, :], jnp.inf, slc)
    score = jnp.where(elig[None, :, None, :], score, -jnp.inf)
    n_top = min(NSA_SEL_COUNT - 1, (q_start + T - 1) // NSA_SEL_BLOCK)
    idx, ok = choose_blocks(score, own, n_top)
    pad = nbs * NSA_SEL_BLOCK - L

    def blocks(a):
        a = jnp.pad(a, ((0, 0), (0, pad), (0, 0), (0, 0)))
        return a.reshape(B, nbs, NSA_SEL_BLOCK, G, D).transpose(0, 3, 1, 2, 4)

    o_sel = gathered_block_attention(q, q_pos, blocks(ks), blocks(vs), idx, ok, tab)
    return o_cmp, o_sel


def local_attention(q, q_pos, k, v, k_pos, tab):
    B, NB, Q, G, J, D = q.shape
    dist = q_pos[:, :, None] - k_pos[:, None, :]
    mask = (k_pos[:, None, :] >= 0) & (dist >= 0) & (dist < NSA_WINDOW)
    bias = tab[rel_bucket(dist)].reshape(dist.shape + (G, J))
    bias = jnp.transpose(bias, (0, 3, 4, 1, 2))
    s = jnp.einsum('bnqgjd,bnkgd->bngjqk', q, k).astype(jnp.float32) * D ** -0.5 + bias[None]
    p = masked_softmax(s, mask[None, :, None, None], axis=-1)
    return jnp.einsum('bngjqk,bnkgd->bnqgjd', p.astype(v.dtype), v)


def window_prompt(q, k, v, tab):
    B, T, G, J, D = q.shape
    qb = NSA_WIN_QBLOCK
    nb = T // qb
    nback = NSA_WINDOW // qb

    def band(a):
        ap = jnp.pad(a, ((0, 0), (nback * qb, 0), (0, 0), (0, 0))).reshape(B, nb + nback, qb, G, D)
        return jnp.concatenate([ap[:, s:s + nb] for s in range(nback + 1)], axis=2)

    blk_off = jnp.arange(nb)[:, None] + jnp.arange(nback + 1)[None, :] - nback
    k_pos = (blk_off[:, :, None] * qb + jnp.arange(qb)).reshape(nb, (nback + 1) * qb)
    q_pos = jnp.arange(T).reshape(nb, qb)
    o = local_attention(q.reshape(B, nb, qb, G, J, D), q_pos, band(k), band(v), k_pos, tab)
    return o.reshape(B, T, G, J, D)


def window_decode(q, q_start, win_buf, win_rows, tab):
    W = win_buf.shape[1]
    T = q.shape[1]
    kv = jnp.concatenate([win_buf, win_rows], axis=1)
    k_pos = (q_start - W + jnp.arange(W + T))[None]
    q_pos = (q_start + jnp.arange(T))[None]
    o = local_attention(q[:, None], q_pos, kv[:, None, :, 0], kv[:, None, :, 1], k_pos, tab)[:, 0]
    return o, kv[:, T:]


def conv_ffn(h, conv_buf, w_gate, w_up, w_down, conv_w, conv_b):
    B, T, _ = h.shape
    g = jnp.einsum('btd,df->btf', h, w_gate)
    u = jnp.einsum('btd,df->btf', h, w_up)
    prev = jnp.zeros((B, CONV_WIDTH - 1, g.shape[-1]), g.dtype) if conv_buf is None else conv_buf
    gp = jnp.concatenate([prev, g], axis=1)
    conv = sum(gp[:, i:i + T] * conv_w[i] for i in range(CONV_WIDTH)) + conv_b
    a = jax.nn.silu(conv) * u
    return jnp.einsum('btf,fd->btd', a, w_down), gp[:, T:]


def decoder_layer(x, q_start, past_moba, past_diff, past_nsa, win_buf, conv_buf, rel_bias, lam_init,
                  g_attn, w_in, lam_p, subln_g, cmp_pos, cmp_w1, cmp_b1, cmp_w2, cmp_b2,
                  w_branch, w_out, g_ffn, w_gate, w_up, w_down, conv_w, conv_b):
    B, T, _ = x.shape
    hd = HEAD_DIM
    h = rmsnorm(x, g_attn)
    proj = jnp.einsum('btd,dp->btp', h, w_in)
    (mq, mk, mv, dq, dk, dv, nq, ck, cv, sk, sv, wk, wv, ng, bg) = split_cols(proj, PROJ_SIZES)

    def heads(a, n, d):
        return a.reshape(B, T, n, d)

    moba_rows = jnp.stack([heads(mk, MOBA_HEADS, hd), heads(mv, MOBA_HEADS, hd)], axis=2)
    diff_rows = jnp.stack([heads(dk, DIFF_HEADS, DIFF_VDIM), heads(dv, DIFF_HEADS, DIFF_VDIM)], axis=2)
    nsa_rows = jnp.stack([heads(a, NSA_GROUPS, hd) for a in (ck, cv, sk, sv)], axis=2)
    win_rows = jnp.stack([heads(wk, NSA_GROUPS, hd), heads(wv, NSA_GROUPS, hd)], axis=2)
    moba_all = moba_rows if past_moba is None else jnp.concatenate([past_moba, moba_rows], axis=1)
    diff_all = diff_rows if past_diff is None else jnp.concatenate([past_diff, diff_rows], axis=1)
    nsa_all = nsa_rows if past_nsa is None else jnp.concatenate([past_nsa, nsa_rows], axis=1)
    L = moba_all.shape[1]
    tab_a = rel_bias[:, :MOBA_HEADS]
    tab_b = rel_bias[:, MOBA_HEADS:MOBA_HEADS + DIFF_HEADS]
    tab_c = rel_bias[:, MOBA_HEADS + DIFF_HEADS:]

    o_a = moba_attention(heads(mq, MOBA_HEADS, hd), q_start, moba_all[:, :, 0], moba_all[:, :, 1], tab_a)

    lam_f = lam_p.astype(jnp.float32)
    lam = jnp.exp(jnp.sum(lam_f[0] * lam_f[1])) - jnp.exp(jnp.sum(lam_f[2] * lam_f[3])) + lam_init
    o_b = diff_attention(dq.reshape(B, T, DIFF_HEADS, 2, hd), q_start,
                         diff_all[:, :, 0].reshape(B, L, DIFF_HEADS, 2, hd), diff_all[:, :, 1], lam, tab_b)
    o_b = rmsnorm(o_b, subln_g, SUBLN_EPS) * (1.0 - lam_init)

    nq = nq.reshape(B, T, NSA_GROUPS, NSA_HPG, hd)
    o_cmp, o_sel = nsa_compressed_selected(nq, q_start, nsa_all[:, :, 0], nsa_all[:, :, 1], nsa_all[:, :, 2],
                                           nsa_all[:, :, 3], cmp_pos, cmp_w1, cmp_b1, cmp_w2, cmp_b2, tab_c)
    if win_buf is None:
        o_win = window_prompt(nq, win_rows[:, :, 0], win_rows[:, :, 1], tab_c)
        new_win = win_rows[:, T - min(NSA_WINDOW, T):]
    else:
        o_win, new_win = window_decode(nq, q_start, win_buf, win_rows, tab_c)
    g_nsa = jax.nn.sigmoid(ng.reshape(B, T, NSA_GROUPS, NSA_HPG, 3))
    o_c = g_nsa[..., 0:1] * o_cmp + g_nsa[..., 1:2] * o_sel + g_nsa[..., 2:3] * o_win

    branches = jnp.stack([o_a.reshape(B, T, BRANCH_WIDTH), o_b.reshape(B, T, BRANCH_WIDTH),
                          o_c.reshape(B, T, BRANCH_WIDTH)], axis=2)
    u = jnp.einsum('btmc,mcd->btmd', branches, w_branch)
    gates = jax.nn.sigmoid(bg.reshape(B, T, N_BRANCH, D_MODEL))
    x = x + jnp.einsum('btd,de->bte', jnp.sum(gates * u, axis=2), w_out)

    f, new_conv = conv_ffn(rmsnorm(x, g_ffn), conv_buf, w_gate, w_up, w_down, conv_w, conv_b)
    return x + f, moba_rows, diff_rows, nsa_rows, new_win, new_conv


def setup_inputs(seed: int = 0) -> dict:
    key = jax.random.key(seed)
    ks = jax.random.split(key, 27)
    f32 = jnp.float32
    n_pages = PAST_LEN // PAGE_SIZE
    n_used = DEC_BATCH * n_pages
    n_pool = n_used + n_used // 4
    win_len = min(NSA_WINDOW, PAST_LEN)

    def nrm(k, shape, scale):
        return jax.random.normal(k, shape, f32) * scale

    page_table = jax.random.permutation(ks[0], n_pool)[:n_used].reshape(DEC_BATCH, n_pages).astype(jnp.int32)
    return {
        'x_prompt': nrm(ks[1], (BATCH, SEQ, D_MODEL), 1.0),
        'x_sample': nrm(ks[2], (DEC_BATCH, DEC_SEQ, D_MODEL), 1.0),
        'cache_moba_kv': nrm(ks[3], (DEPTH, n_pool, PAGE_SIZE, 2, MOBA_HEADS, HEAD_DIM), 1.0),
        'cache_diff_kv': nrm(ks[4], (DEPTH, n_pool, PAGE_SIZE, 2, DIFF_HEADS, DIFF_VDIM), 1.0),
        'cache_nsa_kv': nrm(ks[5], (DEPTH, n_pool, PAGE_SIZE, 4, NSA_GROUPS, HEAD_DIM), 1.0),
        'state_nsa_win': nrm(ks[6], (DEPTH, DEC_BATCH, win_len, 2, NSA_GROUPS, HEAD_DIM), 1.0),
        'state_ffn_conv': nrm(ks[7], (DEPTH, DEC_BATCH, CONV_WIDTH - 1, D_FF), 1.0),
        'page_table': page_table,
        'rel_bias': nrm(ks[8], (REL_BUCKETS, N_BIAS_HEADS), 0.5),
        'norm_attn': 1.0 + nrm(ks[9], (DEPTH, D_MODEL), 0.05),
        'w_in': nrm(ks[10], (DEPTH, D_MODEL, PROJ_WIDTH), D_MODEL ** -0.5),
        'diff_lambda': nrm(ks[11], (DEPTH, 4, HEAD_DIM), 0.1),
        'diff_subln': 1.0 + nrm(ks[12], (DEPTH, DIFF_VDIM), 0.05),
        'nsa_cmp_pos': nrm(ks[13], (DEPTH, 2, NSA_CMP_BLOCK, HEAD_DIM), 0.1),
        'nsa_cmp_w1': nrm(ks[14], (DEPTH, 2, NSA_CMP_BLOCK, HEAD_DIM, NSA_CMP_HIDDEN), (NSA_CMP_BLOCK * HEAD_DIM) ** -0.5),
        'nsa_cmp_b1': nrm(ks[15], (DEPTH, 2, NSA_CMP_HIDDEN), 0.02),
        'nsa_cmp_w2': nrm(ks[16], (DEPTH, 2, NSA_CMP_HIDDEN, HEAD_DIM), NSA_CMP_HIDDEN ** -0.5),
        'nsa_cmp_b2': nrm(ks[17], (DEPTH, 2, HEAD_DIM), 0.02),
        'w_branch': nrm(ks[18], (DEPTH, N_BRANCH, BRANCH_WIDTH, D_MODEL), BRANCH_WIDTH ** -0.5),
        'w_out': nrm(ks[19], (DEPTH, D_MODEL, D_MODEL), D_MODEL ** -0.5),
        'norm_ffn': 1.0 + nrm(ks[20], (DEPTH, D_MODEL), 0.05),
        'w_gate': nrm(ks[21], (DEPTH, D_MODEL, D_FF), D_MODEL ** -0.5),
        'w_up': nrm(ks[22], (DEPTH, D_MODEL, D_FF), D_MODEL ** -0.5),
        'w_down': nrm(ks[23], (DEPTH, D_FF, D_MODEL), D_FF ** -0.5),
        'conv_w': nrm(ks[24], (DEPTH, CONV_WIDTH, D_FF), CONV_WIDTH ** -0.5),
        'conv_b': nrm(ks[25], (DEPTH, D_FF), 0.02),
        'norm_final': 1.0 + nrm(ks[26], (D_MODEL,), 0.05),
    }


def reference(x_prompt, x_sample, cache_moba_kv, cache_diff_kv, cache_nsa_kv, state_nsa_win, state_ffn_conv,
              page_table, rel_bias, norm_attn, w_in, diff_lambda, diff_subln, nsa_cmp_pos, nsa_cmp_w1, nsa_cmp_b1,
              nsa_cmp_w2, nsa_cmp_b2, w_branch, w_out, norm_ffn, w_gate, w_up, w_down, conv_w, conv_b, norm_final):
    past_len = page_table.shape[1] * cache_moba_kv.shape[2]
    y_p, y_s = x_prompt, x_sample
    outs_p = [[] for _ in range(5)]
    outs_s = [[] for _ in range(5)]
    for layer in range(DEPTH):
        lam_init = 0.8 - 0.6 * math.exp(-0.3 * layer)
        weights = (rel_bias, lam_init, norm_attn[layer], w_in[layer], diff_lambda[layer], diff_subln[layer],
                   nsa_cmp_pos[layer], nsa_cmp_w1[layer], nsa_cmp_b1[layer], nsa_cmp_w2[layer], nsa_cmp_b2[layer],
                   w_branch[layer], w_out[layer], norm_ffn[layer], w_gate[layer], w_up[layer], w_down[layer],
                   conv_w[layer], conv_b[layer])
        y_p, *new_p = decoder_layer(y_p, 0, None, None, None, None, None, *weights)
        past_moba = gather_pages(cache_moba_kv[layer], page_table)
        past_diff = gather_pages(cache_diff_kv[layer], page_table)
        past_nsa = gather_pages(cache_nsa_kv[layer], page_table)
        y_s, *new_s = decoder_layer(y_s, past_len, past_moba, past_diff, past_nsa, state_nsa_win[layer],
                                    state_ffn_conv[layer], *weights)
        for lst, a in zip(outs_p, new_p):
            lst.append(a)
        for lst, a in zip(outs_s, new_s):
            lst.append(a)
    y_p = rmsnorm(y_p, norm_final)
    y_s = rmsnorm(y_s, norm_final)
    moba_kv_prompt, diff_kv_prompt, nsa_kv_prompt, nsa_win_prompt, ffn_conv_prompt = [jnp.stack(l) for l in outs_p]
    moba_kv_sample, diff_kv_sample, nsa_kv_sample, nsa_win_sample, ffn_conv_sample = [jnp.stack(l) for l in outs_s]
    return (y_p, y_s, moba_kv_prompt, moba_kv_sample, diff_kv_prompt, diff_kv_sample, nsa_kv_prompt, nsa_kv_sample,
            nsa_win_prompt, nsa_win_sample, ffn_conv_prompt, ffn_conv_sample)
```

```python
import functools
import math

import numpy as np
import jax
import jax.numpy as jnp
from jax import lax
from jax.experimental import pallas as pl
from jax.experimental.pallas import tpu as pltpu

D_MODEL = 4096
HEAD_DIM = 128
MOBA_HEADS = 8
MOBA_BLOCK = 256
MOBA_TOPK = 3
DIFF_HEADS = 4
DIFF_VDIM = 2 * HEAD_DIM
NSA_HEADS = 8
NSA_GROUPS = 2
NSA_HPG = NSA_HEADS // NSA_GROUPS
NSA_CMP_BLOCK = 32
NSA_CMP_STRIDE = 16
NSA_CMP_HIDDEN = 2 * HEAD_DIM
NSA_SEL_BLOCK = 64
NSA_SEL_COUNT = 16
NSA_WINDOW = 512
REL_BUCKETS = 32
REL_MAX_DIST = 128
N_BRANCH = 3
BRANCH_WIDTH = MOBA_HEADS * HEAD_DIM
D_FF = 11008
CONV_WIDTH = 3
RMS_EPS = 1e-6
SUBLN_EPS = 1e-5

_BW = BRANCH_WIDTH
_GW = NSA_GROUPS * HEAD_DIM
C_MQ, C_MK, C_MV = 0, _BW, 2 * _BW
C_DQ, C_DK, C_DV = 3 * _BW, 4 * _BW, 5 * _BW
C_NQ = 6 * _BW
C_CK = 7 * _BW
C_CV, C_SK, C_SV, C_WK, C_WV = (C_CK + _GW, C_CK + 2 * _GW, C_CK + 3 * _GW,
                                C_CK + 4 * _GW, C_CK + 5 * _GW)
C_NG = C_CK + 6 * _GW
N_NG = 3 * NSA_HEADS
C_BG_SRC = C_NG + N_NG
NG_PAD = 512
C_BG = C_NG + NG_PAD
PROJ_PACKED = C_BG + N_BRANCH * D_MODEL
D_FF_PAD = 11264

VMEM_LIMIT = 56 * 1024 * 1024


def _bucket_of_dist():
    n = np.arange(REL_MAX_DIST + 1)
    exact = REL_BUCKETS // 2
    nf = np.maximum(n, 1).astype(np.float64)
    large = exact + (np.log(nf / exact) / math.log(REL_MAX_DIST / exact) * (REL_BUCKETS - exact)).astype(np.int64)
    return np.where(n < exact, n, np.minimum(large, REL_BUCKETS - 1)).astype(np.int32)


_BUCKET = _bucket_of_dist()


def _rmsnorm_kernel(x_ref, g_ref, o_ref, *, eps):
    x = x_ref[...]
    ms = jnp.mean(x * x, axis=-1, keepdims=True)
    o_ref[...] = (x * lax.rsqrt(ms + eps) * g_ref[...]).astype(o_ref.dtype)


def rmsnorm_rows(x, g, eps, out_dtype):
    m, d = x.shape
    tm = min(m, 256)
    return pl.pallas_call(
        functools.partial(_rmsnorm_kernel, eps=eps),
        grid=(m // tm,),
        in_specs=[pl.BlockSpec((tm, d), lambda i: (i, 0)),
                  pl.BlockSpec((1, d), lambda i: (0, 0))],
        out_specs=pl.BlockSpec((tm, d), lambda i: (i, 0)),
        out_shape=jax.ShapeDtypeStruct((m, d), out_dtype),
        compiler_params=pltpu.CompilerParams(dimension_semantics=("parallel",)),
    )(x, g.reshape(1, d).astype(jnp.float32))


def _matmul_kernel(a_ref, w_ref, *rest, nk, has_res):
    if has_res:
        r_ref, o_ref = rest
    else:
        (o_ref,) = rest
    part = jnp.dot(a_ref[...], w_ref[...], preferred_element_type=jnp.float32)
    if nk == 1:
        o_ref[...] = part + r_ref[...] if has_res else part
        return
    k = pl.program_id(2)

    @pl.when(k == 0)
    def _():
        o_ref[...] = part + r_ref[...] if has_res else part

    @pl.when(k > 0)
    def _():
        o_ref[...] += part


def matmul(a, w, res=None, *, tm=512, tn=1024, tk=None):
    m, kd = a.shape
    n = w.shape[1]
    tm = min(tm, m)
    tn = min(tn, n)
    tk = kd if tk is None else tk
    nk = kd // tk
    in_specs = [pl.BlockSpec((tm, tk), lambda j, i, k: (i, k)),
                pl.BlockSpec((tk, tn), lambda j, i, k: (k, j))]
    args = [a, w]
    if res is not None:
        in_specs.append(pl.BlockSpec((tm, tn), lambda j, i, k: (i, j)))
        args.append(res)
    return pl.pallas_call(
        functools.partial(_matmul_kernel, nk=nk, has_res=res is not None),
        grid=(n // tn, m // tm, nk),
        in_specs=in_specs,
        out_specs=pl.BlockSpec((tm, tn), lambda j, i, k: (i, j)),
        out_shape=jax.ShapeDtypeStruct((m, n), jnp.float32),
        compiler_params=pltpu.CompilerParams(
            dimension_semantics=("parallel", "parallel", "arbitrary"),
            vmem_limit_bytes=VMEM_LIMIT),
    )(*args)


def _merge_kernel(br_ref, wb_ref, g0_ref, g1_ref, g2_ref, o_ref):
    acc = None
    for m, g_ref in enumerate((g0_ref, g1_ref, g2_ref)):
        u = jnp.dot(br_ref[m], wb_ref[m], preferred_element_type=jnp.float32)
        t = jax.nn.sigmoid(g_ref[...]) * u
        acc = t if acc is None else acc + t
    o_ref[...] = acc.astype(o_ref.dtype)


def merge_branches(branches, w_branch, proj, *, tm=512, tn=1024):
    _, m, bw = branches.shape
    d = w_branch.shape[2]
    tm = min(tm, m)
    gate_specs = [
        pl.BlockSpec((tm, tn), functools.partial(
            lambda j, i, base: (i, base + j), base=(C_BG + mm * d) // tn))
        for mm in range(N_BRANCH)]
    return pl.pallas_call(
        _merge_kernel,
        grid=(d // tn, m // tm),
        in_specs=[pl.BlockSpec((N_BRANCH, tm, bw), lambda j, i: (0, i, 0)),
                  pl.BlockSpec((N_BRANCH, bw, tn), lambda j, i: (0, 0, j))] + gate_specs,
        out_specs=pl.BlockSpec((tm, tn), lambda j, i: (i, j)),
        out_shape=jax.ShapeDtypeStruct((m, d), jnp.bfloat16),
        compiler_params=pltpu.CompilerParams(
            dimension_semantics=("parallel", "parallel"), vmem_limit_bytes=VMEM_LIMIT),
    )(branches, w_branch, proj, proj, proj)


def _ffn_in_kernel(h_ref, gprev_ref, wg_ref, wu_ref, cw_ref, cb_ref, o_ref):
    h = h_ref[...]
    g = jnp.dot(h, wg_ref[...], preferred_element_type=jnp.float32)
    u = jnp.dot(h, wu_ref[...], preferred_element_type=jnp.float32)
    tm = g.shape[0]
    gcat = jnp.concatenate([gprev_ref[0], g], axis=0)
    cw = cw_ref[...]
    conv = (gcat[6:6 + tm] * cw[0:1] + gcat[7:7 + tm] * cw[1:2] + g * cw[2:3] + cb_ref[...])
    o_ref[...] = (jax.nn.silu(conv) * u).astype(o_ref.dtype)


def ffn_in(h, gprev, wg, wu, conv_w, conv_b, *, tm=512, tf=512):
    m, d = h.shape
    f = wg.shape[1]
    return pl.pallas_call(
        _ffn_in_kernel,
        grid=(f // tf, m // tm),
        in_specs=[pl.BlockSpec((tm, d), lambda j, i: (i, 0)),
                  pl.BlockSpec((1, 8, tf), lambda j, i: (i, 0, j)),
                  pl.BlockSpec((d, tf), lambda j, i: (0, j)),
                  pl.BlockSpec((d, tf), lambda j, i: (0, j)),
                  pl.BlockSpec((CONV_WIDTH, tf), lambda j, i: (0, j)),
                  pl.BlockSpec((1, tf), lambda j, i: (0, j))],
        out_specs=pl.BlockSpec((tm, tf), lambda j, i: (i, j)),
        out_shape=jax.ShapeDtypeStruct((m, f), jnp.bfloat16),
        compiler_params=pltpu.CompilerParams(
            dimension_semantics=("parallel", "parallel"), vmem_limit_bytes=VMEM_LIMIT),
    )(h, gprev, wg, wu, conv_w, conv_b)


def _conv_act_kernel(g0_ref, g1_ref, g2_ref, u_ref, cw_ref, cb_ref, o_ref):
    cw = cw_ref[...]
    conv = g0_ref[...] * cw[0:1] + g1_ref[...] * cw[1:2] + g2_ref[...] * cw[2:3] + cb_ref[...]
    o_ref[...] = (jax.nn.silu(conv) * u_ref[...]).astype(o_ref.dtype)


def conv_act(g0, g1, g2, u, conv_w, conv_b, *, tf=1024):
    m, f = u.shape
    row = pl.BlockSpec((m, tf), lambda j: (0, j))
    return pl.pallas_call(
        _conv_act_kernel,
        grid=(f // tf,),
        in_specs=[row, row, row, row,
                  pl.BlockSpec((CONV_WIDTH, tf), lambda j: (0, j)),
                  pl.BlockSpec((1, tf), lambda j: (0, j))],
        out_specs=row,
        out_shape=jax.ShapeDtypeStruct((m, f), jnp.bfloat16),
        compiler_params=pltpu.CompilerParams(dimension_semantics=("parallel",)),
    )(g0, g1, g2, u, conv_w, conv_b)


def _masked_softmax(s, mask, axis):
    s = jnp.where(mask, s, -jnp.inf)
    m = jnp.max(s, axis=axis, keepdims=True)
    m = jnp.where(jnp.isfinite(m), m, 0.0)
    e = jnp.where(mask, jnp.exp(s - m), 0.0)
    return e / jnp.maximum(jnp.sum(e, axis=axis, keepdims=True), 1e-30)


def _topk_mask(score, n_top):
    nb = score.shape[-1]
    iota = lax.broadcasted_iota(jnp.int32, score.shape, score.ndim - 1)
    sel = jnp.zeros(score.shape, bool)
    for _ in range(n_top):
        m = jnp.max(score, axis=-1, keepdims=True)
        first = jnp.min(jnp.where(score == m, iota, nb), axis=-1, keepdims=True)
        hit = (iota == first) & (m > -jnp.inf)
        sel = sel | hit
        score = jnp.where(iota == first, -jnp.inf, score)
    return sel


def _bias_of_dist(strip, q_pos, k_pos):
    d = jnp.clip(q_pos[:, None] - k_pos[None, :], 0, REL_MAX_DIST)
    return strip[d]


def moba_attention(q, q_start, k, v, strip):
    b, t, h, d = q.shape
    l = k.shape[1]
    nb = -(-l // MOBA_BLOCK)
    pad = nb * MOBA_BLOCK - l
    kp = jnp.pad(k, ((0, 0), (0, pad), (0, 0), (0, 0)))
    k_mean = jnp.mean(kp.reshape(b, nb, MOBA_BLOCK, h, d), axis=2)
    q_pos = q_start + jnp.arange(t)
    k_pos = jnp.arange(l)
    own = q_pos // MOBA_BLOCK
    blk = jnp.arange(nb)
    score = jnp.einsum('bthd,bnhd->bthn', q, k_mean, precision=lax.Precision.HIGHEST)
    score = jnp.where((blk[None, :] < own[:, None])[None, :, None, :], score, -jnp.inf)
    n_top = min(MOBA_TOPK, (q_start + t - 1) // MOBA_BLOCK)
    sel = _topk_mask(score, n_top) | (blk[None, :] == own[:, None])[None, :, None, :]
    kmask = jnp.take(sel, k_pos // MOBA_BLOCK, axis=-1)
    mask = kmask & (q_pos[:, None] >= k_pos[None, :])[None, :, None, :]
    bias = jnp.transpose(_bias_of_dist(strip, q_pos, k_pos), (0, 2, 1))
    s = jnp.einsum('bthd,bshd->bths', q, k) * d ** -0.5 + bias[None]
    p = _masked_softmax(s, mask, -1)
    return jnp.einsum('bths,bshd->bthd', p, v)


def diff_attention(q, q_start, k, v, lam, strip):
    t = q.shape[1]
    l = k.shape[1]
    d = q.shape[-1]
    q_pos = q_start + jnp.arange(t)
    k_pos = jnp.arange(l)
    bias = jnp.transpose(_bias_of_dist(strip, q_pos, k_pos), (2, 0, 1))
    s = jnp.einsum('bthmd,bshmd->bhmts', q, k) * d ** -0.5 + bias[None, :, None]
    p = _masked_softmax(s, (q_pos[:, None] >= k_pos[None, :]), -1)
    a = p[:, :, 0] - lam * p[:, :, 1]
    return jnp.einsum('bhts,bshe->bthe', a, v)


def nsa_compress(k, pos_emb, w1, b1, w2, b2):
    b, l, g, d = k.shape
    r = NSA_CMP_BLOCK // NSA_CMP_STRIDE
    n_ch = l // NSA_CMP_STRIDE
    n_cmp = n_ch - r + 1
    ch = k[:, :n_ch * NSA_CMP_STRIDE].reshape(b, n_ch, NSA_CMP_STRIDE, g, d)
    pe = pos_emb.reshape(r, NSA_CMP_STRIDE, d)
    w1r = w1.reshape(r, NSA_CMP_STRIDE, d, NSA_CMP_HIDDEN)
    pre = b1
    for s in range(r):
        pre = pre + jnp.einsum('bnlgd,ldh->bngh', ch[:, s:s + n_cmp] + pe[s][:, None, :], w1r[s])
    return jnp.einsum('bngh,hd->bngd', jax.nn.gelu(pre), w2) + b2


def nsa_cmp_sel(q, q_start, kc, vc, ks, vs, cmp_pos, cmp_w1, cmp_b1, cmp_w2, cmp_b2, strip):
    b, t, g, j, d = q.shape
    l = kc.shape[1]
    scale = d ** -0.5
    q_pos = q_start + jnp.arange(t)
    k_pos = jnp.arange(l)
    k_cmp = nsa_compress(kc, cmp_pos[0], cmp_w1[0], cmp_b1[0], cmp_w2[0], cmp_b2[0])
    v_cmp = nsa_compress(vc, cmp_pos[1], cmp_w1[1], cmp_b1[1], cmp_w2[1], cmp_b2[1])
    n_cmp = k_cmp.shape[1]
    cmp_end = jnp.arange(n_cmp) * NSA_CMP_STRIDE + NSA_CMP_BLOCK - 1
    valid = cmp_end[None, :] <= q_pos[:, None]
    s = jnp.einsum('btgjd,bngd->bgjtn', q, k_cmp, precision=lax.Precision.HIGHEST) * scale
    p = _masked_softmax(s, valid, -1)
    o_cmp = jnp.einsum('bgjtn,bngd->btgjd', p, v_cmp)
    imp = jnp.sum(p, axis=2)
    nbs = -(-l // NSA_SEL_BLOCK)
    ratio = NSA_SEL_BLOCK // NSA_CMP_STRIDE
    front = NSA_CMP_BLOCK // NSA_CMP_STRIDE - 1
    span = ratio + front
    back = max(ratio * (nbs - 1) + span - front - n_cmp, 0)
    imp = jnp.pad(imp, ((0, 0), (0, 0), (0, 0), (front, back)))
    slc = sum(imp[..., o:o + ratio * nbs:ratio] for o in range(span))
    slc = jnp.transpose(slc, (0, 2, 1, 3))
    own = q_pos // NSA_SEL_BLOCK
    blk = jnp.arange(nbs)
    elig = blk[None, :] < own[:, None]
    forced = (blk[None, :] == 0) | (blk[None, :] == own[:, None] - 1)
    score = jnp.where(forced[None, :, None, :], jnp.inf, slc)
    score = jnp.where(elig[None, :, None, :], score, -jnp.inf)
    n_top = min(NSA_SEL_COUNT - 1, (q_start + t - 1) // NSA_SEL_BLOCK)
    sel = _topk_mask(score, n_top) | (blk[None, :] == own[:, None])[None, :, None, :]
    kmask = jnp.take(sel, k_pos // NSA_SEL_BLOCK, axis=-1)
    mask = kmask & (q_pos[:, None] >= k_pos[None, :])[None, :, None, :]
    mask = jnp.transpose(mask, (0, 2, 1, 3))[:, :, None]
    bias = _bias_of_dist(strip, q_pos, k_pos).reshape(t, l, g, j)
    bias = jnp.transpose(bias, (2, 3, 0, 1))
    s2 = jnp.einsum('btgjd,bsgd->bgjts', q, ks) * scale + bias[None]
    p2 = _masked_softmax(s2, mask, -1)
    o_sel = jnp.einsum('bgjts,bsgd->btgjd', p2, vs)
    return o_cmp, o_sel


def window_attention(q, q_pos, k, v, k_pos, strip):
    b, t, g, j, d = q.shape
    l = k.shape[1]
    dist = q_pos[:, None] - k_pos[None, :]
    mask = (k_pos[None, :] >= 0) & (dist >= 0) & (dist < NSA_WINDOW)
    bias = jnp.transpose(_bias_of_dist(strip, q_pos, k_pos).reshape(t, l, g, j), (2, 3, 0, 1))
    s = jnp.einsum('btgjd,bsgd->bgjts', q, k) * d ** -0.5 + bias[None]
    p = _masked_softmax(s, mask, -1)
    return jnp.einsum('bgjts,bsgd->btgjd', p, v)


def _layer(x, q_start, past, win_buf, conv_buf, strip, lam_init, wts):
    (g_attn, w_in_p, lam_p, subln_g, cmp_pos, cmp_w1, cmp_b1, cmp_w2, cmp_b2,
     w_branch, w_out, g_ffn, w_gate, w_up, w_down, conv_w, conv_b) = wts
    b, t, _ = x.shape
    m = b * t
    hd = HEAD_DIM
    x2 = x.reshape(m, D_MODEL)
    h = rmsnorm_rows(x2, g_attn, RMS_EPS, jnp.bfloat16)
    proj = matmul(h, w_in_p)
    p3 = proj.reshape(b, t, PROJ_PACKED)

    def cols(start, width):
        return p3[:, :, start:start + width]

    moba_rows = cols(C_MK, 2 * _BW).reshape(b, t, 2, MOBA_HEADS, hd)
    diff_rows = cols(C_DK, 2 * _BW).reshape(b, t, 2, DIFF_HEADS, DIFF_VDIM)
    nsa_rows = cols(C_CK, 4 * _GW).reshape(b, t, 4, NSA_GROUPS, hd)
    win_rows = cols(C_WK, 2 * _GW).reshape(b, t, 2, NSA_GROUPS, hd)
    if past is None:
        moba_all, diff_all, nsa_all = moba_rows, diff_rows, nsa_rows
    else:
        moba_all = jnp.concatenate([past[0], moba_rows], axis=1)
        diff_all = jnp.concatenate([past[1], diff_rows], axis=1)
        nsa_all = jnp.concatenate([past[2], nsa_rows], axis=1)
    l = moba_all.shape[1]
    strip_a = strip[:, :MOBA_HEADS]
    strip_b = strip[:, MOBA_HEADS:MOBA_HEADS + DIFF_HEADS]
    strip_c = strip[:, MOBA_HEADS + DIFF_HEADS:]

    o_a = moba_attention(cols(C_MQ, _BW).reshape(b, t, MOBA_HEADS, hd), q_start,
                         moba_all[:, :, 0], moba_all[:, :, 1], strip_a)

    lam_f = lam_p.astype(jnp.float32)
    lam = jnp.exp(jnp.sum(lam_f[0] * lam_f[1])) - jnp.exp(jnp.sum(lam_f[2] * lam_f[3])) + lam_init
    o_b = diff_attention(cols(C_DQ, _BW).reshape(b, t, DIFF_HEADS, 2, hd), q_start,
                         diff_all[:, :, 0].reshape(b, l, DIFF_HEADS, 2, hd), diff_all[:, :, 1], lam, strip_b)
    o_b = o_b * lax.rsqrt(jnp.mean(o_b * o_b, axis=-1, keepdims=True) + SUBLN_EPS) * subln_g * (1.0 - lam_init)

    nq = cols(C_NQ, _BW).reshape(b, t, NSA_GROUPS, NSA_HPG, hd)
    o_cmp, o_sel = nsa_cmp_sel(nq, q_start, nsa_all[:, :, 0], nsa_all[:, :, 1], nsa_all[:, :, 2],
                               nsa_all[:, :, 3], cmp_pos, cmp_w1, cmp_b1, cmp_w2, cmp_b2, strip_c)
    q_pos = q_start + jnp.arange(t)
    if win_buf is None:
        o_win = window_attention(nq, q_pos, win_rows[:, :, 0], win_rows[:, :, 1], jnp.arange(t), strip_c)
        new_win = win_rows[:, t - min(NSA_WINDOW, t):]
    else:
        w = win_buf.shape[1]
        kv = jnp.concatenate([win_buf, win_rows], axis=1)
        o_win = window_attention(nq, q_pos, kv[:, :, 0], kv[:, :, 1], q_start - w + jnp.arange(w + t), strip_c)
        new_win = kv[:, t:]
    g_nsa = jax.nn.sigmoid(cols(C_NG, N_NG).reshape(b, t, NSA_GROUPS, NSA_HPG, 3))
    o_c = g_nsa[..., 0:1] * o_cmp + g_nsa[..., 1:2] * o_sel + g_nsa[..., 2:3] * o_win

    branches = jnp.stack([o_a.reshape(m, _BW), o_b.reshape(m, _BW), o_c.reshape(m, _BW)]).astype(jnp.bfloat16)
    mix = merge_branches(branches, w_branch, proj)
    x1 = matmul(mix, w_out, x2)

    hf = rmsnorm_rows(x1, g_ffn, RMS_EPS, jnp.bfloat16)
    if conv_buf is None:
        tm = 512
        hprev = hf.reshape(b, t // tm, tm, D_MODEL)[:, :, tm - 8:, :]
        hprev = jnp.concatenate([jnp.zeros_like(hprev[:, :1]), hprev[:, :-1]], axis=1)
        gprev = matmul(hprev.reshape(m // tm * 8, D_MODEL), w_gate).reshape(m // tm, 8, D_FF_PAD)
        act = ffn_in(hf, gprev, w_gate, w_up, conv_w, conv_b, tm=tm)
        g_last = matmul(hf.reshape(b, t, D_MODEL)[:, t - 8:].reshape(b * 8, D_MODEL), w_gate)
        new_conv = g_last.reshape(b, 8, D_FF_PAD)[:, 8 - (CONV_WIDTH - 1):, :D_FF]
    else:
        g = matmul(hf, w_gate).reshape(b, t, D_FF_PAD)
        u = matmul(hf, w_up)
        prev = jnp.pad(conv_buf, ((0, 0), (0, 0), (0, D_FF_PAD - D_FF)))
        gp = jnp.concatenate([prev, g], axis=1)
        shifted = [gp[:, i:i + t].reshape(m, D_FF_PAD) for i in range(CONV_WIDTH)]
        act = conv_act(*shifted, u, conv_w, conv_b)
        new_conv = gp[:, t:, :D_FF]
    x2o = matmul(act, w_down, x1, tm=1024, tn=1024, tk=2816)
    return x2o.reshape(b, t, D_MODEL), moba_rows, diff_rows, nsa_rows, new_win, new_conv


def _pack_w_in(w_in):
    head = w_in[:, :, :C_BG_SRC]
    head = jnp.pad(head, ((0, 0), (0, 0), (0, NG_PAD - N_NG)))
    return jnp.concatenate([head, w_in[:, :, C_BG_SRC:]], axis=2).astype(jnp.bfloat16)


def kernel(x_prompt, x_sample, cache_moba_kv, cache_diff_kv, cache_nsa_kv, state_nsa_win, state_ffn_conv, page_table, rel_bias, norm_attn, w_in, diff_lambda, diff_subln, nsa_cmp_pos, nsa_cmp_w1, nsa_cmp_b1, nsa_cmp_w2, nsa_cmp_b2, w_branch, w_out, norm_ffn, w_gate, w_up, w_down, conv_w, conv_b, norm_final):
    depth = w_in.shape[0]
    past_len = page_table.shape[1] * cache_moba_kv.shape[2]
    bf = jnp.bfloat16
    fpad = D_FF_PAD - D_FF
    w_in_p = _pack_w_in(w_in)
    w_branch_b = w_branch.astype(bf)
    w_out_b = w_out.astype(bf)
    w_gate_b = jnp.pad(w_gate, ((0, 0), (0, 0), (0, fpad))).astype(bf)
    w_up_b = jnp.pad(w_up, ((0, 0), (0, 0), (0, fpad))).astype(bf)
    w_down_b = jnp.pad(w_down, ((0, 0), (0, fpad), (0, 0))).astype(bf)
    conv_w_p = jnp.pad(conv_w, ((0, 0), (0, 0), (0, fpad)))
    conv_b_p = jnp.pad(conv_b, ((0, 0), (0, fpad))).reshape(depth, 1, D_FF_PAD)
    strip = rel_bias[jnp.asarray(_BUCKET)]

    y_p, y_s = x_prompt, x_sample
    outs_p = [[] for _ in range(5)]
    outs_s = [[] for _ in range(5)]
    for layer in range(depth):
        lam_init = 0.8 - 0.6 * math.exp(-0.3 * layer)
        wts = (norm_attn[layer], w_in_p[layer], diff_lambda[layer], diff_subln[layer], nsa_cmp_pos[layer],
               nsa_cmp_w1[layer], nsa_cmp_b1[layer], nsa_cmp_w2[layer], nsa_cmp_b2[layer], w_branch_b[layer],
               w_out_b[layer], norm_ffn[layer], w_gate_b[layer], w_up_b[layer], w_down_b[layer],
               conv_w_p[layer], conv_b_p[layer])
        y_p, *new_p = _layer(y_p, 0, None, None, None, strip, lam_init, wts)

        def gather(pool):
            rows = pool[page_table]
            return rows.reshape((rows.shape[0], rows.shape[1] * rows.shape[2]) + rows.shape[3:])

        past = (gather(cache_moba_kv[layer]), gather(cache_diff_kv[layer]), gather(cache_nsa_kv[layer]))
        y_s, *new_s = _layer(y_s, past_len, past, state_nsa_win[layer], state_ffn_conv[layer], strip, lam_init, wts)
        for lst, a in zip(outs_p, new_p):
            lst.append(a)
        for lst, a in zip(outs_s, new_s):
            lst.append(a)
    bp, tp, _ = y_p.shape
    bs, ts, _ = y_s.shape
    y_p = rmsnorm_rows(y_p.reshape(bp * tp, D_MODEL), norm_final, RMS_EPS, jnp.float32).reshape(bp, tp, D_MODEL)
    y_s = rmsnorm_rows(y_s.reshape(bs * ts, D_MODEL), norm_final, RMS_EPS, jnp.float32).reshape(bs, ts, D_MODEL)
    sp = [jnp.stack(l) for l in outs_p]
    ss = [jnp.stack(l) for l in outs_s]
    return (y_p, y_s, sp[0], ss[0], sp[1], ss[1], sp[2], ss[2], sp[3], ss[3], sp[4], ss[4])
```

```python
import functools
import math

import numpy as np
import jax
import jax.numpy as jnp
from jax import lax
from jax.experimental import pallas as pl
from jax.experimental.pallas import tpu as pltpu

D_MODEL = 4096
HEAD_DIM = 128
MOBA_HEADS = 8
MOBA_BLOCK = 256
MOBA_TOPK = 3
DIFF_HEADS = 4
DIFF_VDIM = 2 * HEAD_DIM
NSA_HEADS = 8
NSA_GROUPS = 2
NSA_HPG = NSA_HEADS // NSA_GROUPS
NSA_CMP_BLOCK = 32
NSA_CMP_STRIDE = 16
NSA_CMP_HIDDEN = 2 * HEAD_DIM
NSA_SEL_BLOCK = 64
NSA_SEL_COUNT = 16
NSA_WINDOW = 512
REL_BUCKETS = 32
REL_MAX_DIST = 128
N_BRANCH = 3
BRANCH_WIDTH = MOBA_HEADS * HEAD_DIM
D_FF = 11008
CONV_WIDTH = 3
RMS_EPS = 1e-6
SUBLN_EPS = 1e-5

_BW = BRANCH_WIDTH
_GW = NSA_GROUPS * HEAD_DIM
C_MQ, C_MK, C_MV = 0, _BW, 2 * _BW
C_DQ, C_DK, C_DV = 3 * _BW, 4 * _BW, 5 * _BW
C_NQ = 6 * _BW
C_CK = 7 * _BW
C_CV, C_SK, C_SV, C_WK, C_WV = (C_CK + _GW, C_CK + 2 * _GW, C_CK + 3 * _GW,
                                C_CK + 4 * _GW, C_CK + 5 * _GW)
C_NG = C_CK + 6 * _GW
N_NG = 3 * NSA_HEADS
C_BG_SRC = C_NG + N_NG
NG_PAD = 512
C_BG = C_NG + NG_PAD
PROJ_PACKED = C_BG + N_BRANCH * D_MODEL
D_FF_PAD = 11264

VMEM_LIMIT = 56 * 1024 * 1024


def _bucket_of_dist():
    n = np.arange(REL_MAX_DIST + 1)
    exact = REL_BUCKETS // 2
    nf = np.maximum(n, 1).astype(np.float64)
    large = exact + (np.log(nf / exact) / math.log(REL_MAX_DIST / exact) * (REL_BUCKETS - exact)).astype(np.int64)
    return np.where(n < exact, n, np.minimum(large, REL_BUCKETS - 1)).astype(np.int32)


_BUCKET = _bucket_of_dist()


def _rmsnorm_kernel(x_ref, g_ref, o_ref, *, eps):
    x = x_ref[...]
    ms = jnp.mean(x * x, axis=-1, keepdims=True)
    o_ref[...] = (x * lax.rsqrt(ms + eps) * g_ref[...]).astype(o_ref.dtype)


def rmsnorm_rows(x, g, eps, out_dtype):
    m, d = x.shape
    tm = min(m, 256)
    return pl.pallas_call(
        functools.partial(_rmsnorm_kernel, eps=eps),
        grid=(m // tm,),
        in_specs=[pl.BlockSpec((tm, d), lambda i: (i, 0)),
                  pl.BlockSpec((1, d), lambda i: (0, 0))],
        out_specs=pl.BlockSpec((tm, d), lambda i: (i, 0)),
        out_shape=jax.ShapeDtypeStruct((m, d), out_dtype),
        compiler_params=pltpu.CompilerParams(dimension_semantics=("parallel",)),
    )(x, g.reshape(1, d).astype(jnp.float32))


def _matmul_kernel(a_ref, w_ref, *rest, nk, has_res):
    if has_res:
        r_ref, o_ref = rest
    else:
        (o_ref,) = rest
    part = jnp.dot(a_ref[...], w_ref[...], preferred_element_type=jnp.float32)
    if nk == 1:
        o_ref[...] = part + r_ref[...] if has_res else part
        return
    k = pl.program_id(2)

    @pl.when(k == 0)
    def _():
        o_ref[...] = part + r_ref[...] if has_res else part

    @pl.when(k > 0)
    def _():
        o_ref[...] += part


def matmul(a, w, res=None, *, tm=512, tn=1024, tk=None):
    m, kd = a.shape
    n = w.shape[1]
    tm = min(tm, m)
    tn = min(tn, n)
    tk = kd if tk is None else tk
    nk = kd // tk
    in_specs = [pl.BlockSpec((tm, tk), lambda j, i, k: (i, k)),
                pl.BlockSpec((tk, tn), lambda j, i, k: (k, j))]
    args = [a, w]
    if res is not None:
        in_specs.append(pl.BlockSpec((tm, tn), lambda j, i, k: (i, j)))
        args.append(res)
    return pl.pallas_call(
        functools.partial(_matmul_kernel, nk=nk, has_res=res is not None),
        grid=(n // tn, m // tm, nk),
        in_specs=in_specs,
        out_specs=pl.BlockSpec((tm, tn), lambda j, i, k: (i, j)),
        out_shape=jax.ShapeDtypeStruct((m, n), jnp.float32),
        compiler_params=pltpu.CompilerParams(
            dimension_semantics=("parallel", "parallel", "arbitrary"),
            vmem_limit_bytes=VMEM_LIMIT),
    )(*args)


def _merge_kernel(b0_ref, b1_ref, b2_ref, wb_ref, g0_ref, g1_ref, g2_ref, o_ref):
    acc = None
    for m, (b_ref, g_ref) in enumerate(((b0_ref, g0_ref), (b1_ref, g1_ref), (b2_ref, g2_ref))):
        u = jnp.dot(b_ref[...], wb_ref[m], preferred_element_type=jnp.float32)
        t = jax.nn.sigmoid(g_ref[...]) * u
        acc = t if acc is None else acc + t
    o_ref[...] = acc.astype(o_ref.dtype)


def merge_branches(branches, w_branch, proj, *, tm=512, tn=1024):
    m, bw = branches[0].shape
    d = w_branch.shape[2]
    tm = min(tm, m)
    gate_specs = [
        pl.BlockSpec((tm, tn), functools.partial(
            lambda j, i, base: (i, base + j), base=(C_BG + mm * d) // tn))
        for mm in range(N_BRANCH)]
    branch_spec = pl.BlockSpec((tm, bw), lambda j, i: (i, 0))
    return pl.pallas_call(
        _merge_kernel,
        grid=(d // tn, m // tm),
        in_specs=[branch_spec] * N_BRANCH + [pl.BlockSpec((N_BRANCH, bw, tn), lambda j, i: (0, 0, j))] + gate_specs,
        out_specs=pl.BlockSpec((tm, tn), lambda j, i: (i, j)),
        out_shape=jax.ShapeDtypeStruct((m, d), jnp.bfloat16),
        compiler_params=pltpu.CompilerParams(
            dimension_semantics=("parallel", "parallel"), vmem_limit_bytes=VMEM_LIMIT),
    )(*branches, w_branch, proj, proj, proj)


def _ffn_in_kernel(h_ref, gprev_ref, wg_ref, wu_ref, cw_ref, cb_ref, o_ref):
    h = h_ref[...]
    g = jnp.dot(h, wg_ref[...], preferred_element_type=jnp.float32)
    u = jnp.dot(h, wu_ref[...], preferred_element_type=jnp.float32)
    tm = g.shape[0]
    gcat = jnp.concatenate([gprev_ref[0], g], axis=0)
    cw = cw_ref[...]
    conv = (gcat[6:6 + tm] * cw[0:1] + gcat[7:7 + tm] * cw[1:2] + g * cw[2:3] + cb_ref[...])
    o_ref[...] = (jax.nn.silu(conv) * u).astype(o_ref.dtype)


def ffn_in(h, gprev, wg, wu, conv_w, conv_b, *, tm=512, tf=512):
    m, d = h.shape
    f = wg.shape[1]
    return pl.pallas_call(
        _ffn_in_kernel,
        grid=(f // tf, m // tm),
        in_specs=[pl.BlockSpec((tm, d), lambda j, i: (i, 0)),
                  pl.BlockSpec((1, 8, tf), lambda j, i: (i, 0, j)),
                  pl.BlockSpec((d, tf), lambda j, i: (0, j)),
                  pl.BlockSpec((d, tf), lambda j, i: (0, j)),
                  pl.BlockSpec((CONV_WIDTH, tf), lambda j, i: (0, j)),
                  pl.BlockSpec((1, tf), lambda j, i: (0, j))],
        out_specs=pl.BlockSpec((tm, tf), lambda j, i: (i, j)),
        out_shape=jax.ShapeDtypeStruct((m, f), jnp.bfloat16),
        compiler_params=pltpu.CompilerParams(
            dimension_semantics=("parallel", "parallel"), vmem_limit_bytes=VMEM_LIMIT),
    )(h, gprev, wg, wu, conv_w, conv_b)


def _conv_act_kernel(g0_ref, g1_ref, g2_ref, u_ref, cw_ref, cb_ref, o_ref):
    cw = cw_ref[...]
    conv = g0_ref[...] * cw[0:1] + g1_ref[...] * cw[1:2] + g2_ref[...] * cw[2:3] + cb_ref[...]
    o_ref[...] = (jax.nn.silu(conv) * u_ref[...]).astype(o_ref.dtype)


def conv_act(g0, g1, g2, u, conv_w, conv_b, *, tf=1024):
    m, f = u.shape
    row = pl.BlockSpec((m, tf), lambda j: (0, j))
    return pl.pallas_call(
        _conv_act_kernel,
        grid=(f // tf,),
        in_specs=[row, row, row, row,
                  pl.BlockSpec((CONV_WIDTH, tf), lambda j: (0, j)),
                  pl.BlockSpec((1, tf), lambda j: (0, j))],
        out_specs=row,
        out_shape=jax.ShapeDtypeStruct((m, f), jnp.bfloat16),
        compiler_params=pltpu.CompilerParams(dimension_semantics=("parallel",)),
    )(g0, g1, g2, u, conv_w, conv_b)


ATT_TILE = 256
N_BIAS_TILES = 3
_HI = lax.Precision.HIGHEST
_NT = (((1,), (1,)), ((), ()))


def _bias_tiles(strip):
    r = np.arange(ATT_TILE)
    d = np.stack([np.clip(o * ATT_TILE + r[:, None] - r[None, :], 0, REL_MAX_DIST) for o in range(N_BIAS_TILES)])
    return jnp.transpose(strip[jnp.asarray(d)], (3, 0, 1, 2))


def _tile_iotas():
    r = lax.broadcasted_iota(jnp.int32, (ATT_TILE, ATT_TILE), 0)
    c = lax.broadcasted_iota(jnp.int32, (ATT_TILE, ATT_TILE), 1)
    return r, c


def _causal_tile():
    r, c = _tile_iotas()
    return (r >= c).astype(jnp.float32)


def _attend(nblk, cond, score_fn, v_fn, s_ref, m_ref, l_ref, acc_ref):
    t = ATT_TILE
    m_ref[...] = jnp.full(m_ref.shape, -jnp.inf, jnp.float32)
    for n in range(nblk):
        @pl.when(cond(n))
        def _(n=n):
            s = score_fn(n)
            s_ref[:, n * t:(n + 1) * t] = s
            m_ref[...] = jnp.maximum(m_ref[...], jnp.max(s, axis=-1, keepdims=True))
    m = m_ref[...]
    m = jnp.where(m == -jnp.inf, 0.0, m)
    l_ref[...] = jnp.zeros(l_ref.shape, jnp.float32)
    acc_ref[...] = jnp.zeros(acc_ref.shape, jnp.float32)
    for n in range(nblk):
        @pl.when(cond(n))
        def _(n=n):
            e = jnp.exp(s_ref[:, n * t:(n + 1) * t] - m)
            l_ref[...] += jnp.sum(e, axis=-1, keepdims=True)
            acc_ref[...] += jnp.dot(e.astype(jnp.bfloat16), v_fn(n), preferred_element_type=jnp.float32)
    return acc_ref[...] / jnp.maximum(l_ref[...], 1e-30)


def _select_top(score, n_top):
    lane = lax.broadcasted_iota(jnp.int32, score.shape, 1)
    sel = jnp.zeros(score.shape, jnp.float32)
    for _ in range(n_top):
        m = jnp.max(score, axis=-1, keepdims=True)
        first = jnp.min(jnp.where(score == m, lane, score.shape[1]), axis=-1, keepdims=True)
        hit = lane == first
        sel = jnp.where(hit & (m > -jnp.inf), 1.0, sel)
        score = jnp.where(hit, -jnp.inf, score)
    return sel


def _moba_prompt_kernel(q_ref, k_ref, v_ref, bias_ref, o_ref, kmean_ref, s_ref, m_ref, l_ref, acc_ref, *, nblk):
    t = ATT_TILE
    qi = pl.program_id(2)

    @pl.when(qi == 0)
    def _():
        kmean_ref[...] = jnp.zeros(kmean_ref.shape, jnp.float32)
        for n in range(nblk):
            kmean_ref[n:n + 1, :] = jnp.mean(k_ref[n * t:(n + 1) * t, :], axis=0, keepdims=True)

    q = q_ref[...]
    lane = lax.broadcasted_iota(jnp.int32, (t, HEAD_DIM), 1)
    score = lax.dot_general(q, kmean_ref[...], _NT, precision=_HI, preferred_element_type=jnp.float32)
    score = jnp.where(lane < qi, score, -jnp.inf)
    sel = _select_top(score, min(MOBA_TOPK, nblk - 1))
    q_bf = q.astype(jnp.bfloat16)
    causal = _causal_tile()
    scale = HEAD_DIM ** -0.5

    def score_fn(n):
        s = lax.dot_general(q_bf, k_ref[n * t:(n + 1) * t, :].astype(jnp.bfloat16), _NT,
                            preferred_element_type=jnp.float32)
        s = s * scale + bias_ref[jnp.minimum(qi - n, N_BIAS_TILES - 1)]
        keep = jnp.where(n == qi, causal, jnp.broadcast_to(sel[:, n:n + 1], (t, t)))
        return jnp.where(keep > 0.5, s, -jnp.inf)

    o = _attend(nblk, lambda n: n <= qi, score_fn,
                lambda n: v_ref[n * t:(n + 1) * t, :].astype(jnp.bfloat16), s_ref, m_ref, l_ref, acc_ref)
    o_ref[...] = o.astype(o_ref.dtype)


def moba_prompt(p3, bias_tiles):
    b, t, _ = p3.shape
    nblk = t // ATT_TILE
    hd = HEAD_DIM
    return pl.pallas_call(
        functools.partial(_moba_prompt_kernel, nblk=nblk),
        grid=(b, MOBA_HEADS, nblk),
        in_specs=[pl.BlockSpec((None, ATT_TILE, hd), lambda bi, h, qi: (bi, qi, C_MQ // hd + h)),
                  pl.BlockSpec((None, t, hd), lambda bi, h, qi: (bi, 0, C_MK // hd + h)),
                  pl.BlockSpec((None, t, hd), lambda bi, h, qi: (bi, 0, C_MV // hd + h)),
                  pl.BlockSpec((None, N_BIAS_TILES, ATT_TILE, ATT_TILE), lambda bi, h, qi: (h, 0, 0, 0))],
        out_specs=pl.BlockSpec((None, ATT_TILE, hd), lambda bi, h, qi: (bi, qi, h)),
        out_shape=jax.ShapeDtypeStruct((b, t, BRANCH_WIDTH), jnp.bfloat16),
        scratch_shapes=[pltpu.VMEM((hd, hd), jnp.float32),
                        pltpu.VMEM((ATT_TILE, t), jnp.float32),
                        pltpu.VMEM((ATT_TILE, 1), jnp.float32),
                        pltpu.VMEM((ATT_TILE, 1), jnp.float32),
                        pltpu.VMEM((ATT_TILE, hd), jnp.float32)],
        compiler_params=pltpu.CompilerParams(
            dimension_semantics=("parallel", "parallel", "arbitrary"), vmem_limit_bytes=VMEM_LIMIT),
    )(p3, p3, p3, bias_tiles)


def _diff_prompt_kernel(lam_ref, q_ref, k_ref, v_ref, bias_ref, g_ref, o_ref, s_ref, m_ref, l_ref, acc_ref,
                        *, nblk, out_scale):
    t = ATT_TILE
    qi = pl.program_id(2)
    causal = _causal_tile()
    scale = HEAD_DIM ** -0.5
    outs = []
    for mp in range(2):
        q_bf = q_ref[:, mp * HEAD_DIM:(mp + 1) * HEAD_DIM].astype(jnp.bfloat16)

        def score_fn(n, mp=mp, q_bf=q_bf):
            kb = k_ref[n * t:(n + 1) * t, mp * HEAD_DIM:(mp + 1) * HEAD_DIM].astype(jnp.bfloat16)
            s = lax.dot_general(q_bf, kb, _NT, preferred_element_type=jnp.float32)
            s = s * scale + bias_ref[jnp.minimum(qi - n, N_BIAS_TILES - 1)]
            return jnp.where(jnp.where(n < qi, 1.0, causal) > 0.5, s, -jnp.inf)

        outs.append(_attend(nblk, lambda n: n <= qi, score_fn,
                            lambda n: v_ref[n * t:(n + 1) * t, :].astype(jnp.bfloat16),
                            s_ref, m_ref, l_ref, acc_ref))
    o = outs[0] - lam_ref[0] * outs[1]
    o = o * lax.rsqrt(jnp.mean(o * o, axis=-1, keepdims=True) + SUBLN_EPS) * g_ref[...] * out_scale
    o_ref[...] = o.astype(o_ref.dtype)


def diff_prompt(p3, bias_tiles, lam, subln_g, lam_init):
    b, t, _ = p3.shape
    nblk = t // ATT_TILE
    w = DIFF_VDIM
    return pl.pallas_call(
        functools.partial(_diff_prompt_kernel, nblk=nblk, out_scale=1.0 - lam_init),
        grid=(b, DIFF_HEADS, nblk),
        in_specs=[pl.BlockSpec(memory_space=pltpu.SMEM),
                  pl.BlockSpec((None, ATT_TILE, w), lambda bi, h, qi: (bi, qi, C_DQ // w + h)),
                  pl.BlockSpec((None, t, w), lambda bi, h, qi: (bi, 0, C_DK // w + h)),
                  pl.BlockSpec((None, t, w), lambda bi, h, qi: (bi, 0, C_DV // w + h)),
                  pl.BlockSpec((None, N_BIAS_TILES, ATT_TILE, ATT_TILE), lambda bi, h, qi: (h, 0, 0, 0)),
                  pl.BlockSpec((1, w), lambda bi, h, qi: (0, 0))],
        out_specs=pl.BlockSpec((None, ATT_TILE, w), lambda bi, h, qi: (bi, qi, h)),
        out_shape=jax.ShapeDtypeStruct((b, t, BRANCH_WIDTH), jnp.bfloat16),
        scratch_shapes=[pltpu.VMEM((ATT_TILE, t), jnp.float32),
                        pltpu.VMEM((ATT_TILE, 1), jnp.float32),
                        pltpu.VMEM((ATT_TILE, 1), jnp.float32),
                        pltpu.VMEM((ATT_TILE, w), jnp.float32)],
        compiler_params=pltpu.CompilerParams(
            dimension_semantics=("parallel", "parallel", "parallel"), vmem_limit_bytes=VMEM_LIMIT),
    )(lam.reshape(1).astype(jnp.float32), p3, p3, p3, bias_tiles, subln_g.reshape(1, w).astype(jnp.float32))


def _gelu_tanh(x):
    return 0.5 * x * (1.0 + jnp.tanh(math.sqrt(2.0 / math.pi) * (x + 0.044715 * (x * x * x))))


def _compress_prompt_kernel(x_ref, pe_ref, w1_ref, b1_ref, w2_ref, b2_ref, o_ref, *, n_ch):
    st = NSA_CMP_STRIDE
    pre_a = jnp.zeros((n_ch, NSA_CMP_HIDDEN), jnp.float32)
    pre_b = jnp.zeros((n_ch, NSA_CMP_HIDDEN), jnp.float32)
    for l in range(st):
        x_l = x_ref[pl.ds(l, n_ch, stride=st), :]
        pre_a += jnp.dot((x_l + pe_ref[l:l + 1, :]).astype(jnp.bfloat16), w1_ref[l],
                         preferred_element_type=jnp.float32)
        pre_b += jnp.dot((x_l + pe_ref[st + l:st + l + 1, :]).astype(jnp.bfloat16), w1_ref[st + l],
                         preferred_element_type=jnp.float32)
    pre = pre_a + pltpu.roll(pre_b, n_ch - 1, 0) + b1_ref[...]
    o_ref[...] = jnp.dot(_gelu_tanh(pre).astype(jnp.bfloat16), w2_ref[...],
                         preferred_element_type=jnp.float32) + b2_ref[...]


def compress_prompt(p3, cmp_pos, cmp_w1, cmp_b1, cmp_w2, cmp_b2):
    b, t, _ = p3.shape
    n_ch = t // NSA_CMP_STRIDE
    hd = HEAD_DIM
    g_n = NSA_GROUPS
    return pl.pallas_call(
        functools.partial(_compress_prompt_kernel, n_ch=n_ch),
        grid=(b, 2, g_n),
        in_specs=[pl.BlockSpec((None, t, hd), lambda bi, kv, g: (bi, 0, C_CK // hd + kv * g_n + g)),
                  pl.BlockSpec((None, NSA_CMP_BLOCK, hd), lambda bi, kv, g: (kv, 0, 0)),
                  pl.BlockSpec((None, NSA_CMP_BLOCK, hd, NSA_CMP_HIDDEN), lambda bi, kv, g: (kv, 0, 0, 0)),
                  pl.BlockSpec((None, 1, NSA_CMP_HIDDEN), lambda bi, kv, g: (kv, 0, 0)),
                  pl.BlockSpec((None, NSA_CMP_HIDDEN, hd), lambda bi, kv, g: (kv, 0, 0)),
                  pl.BlockSpec((None, 1, hd), lambda bi, kv, g: (kv, 0, 0))],
        out_specs=pl.BlockSpec((None, None, None, n_ch, hd), lambda bi, kv, g: (bi, kv, g, 0, 0)),
        out_shape=jax.ShapeDtypeStruct((b, 2, g_n, n_ch, hd), jnp.float32),
        compiler_params=pltpu.CompilerParams(
            dimension_semantics=("parallel", "parallel", "parallel"), vmem_limit_bytes=VMEM_LIMIT),
    )(p3, cmp_pos, cmp_w1.astype(jnp.bfloat16), cmp_b1.reshape(2, 1, NSA_CMP_HIDDEN),
      cmp_w2.astype(jnp.bfloat16), cmp_b2.reshape(2, 1, hd))


def _cmp_select_prompt_kernel(q_ref, kc_ref, vc_ref, o_ref, sel_ref, *, n_sel_blocks):
    t = ATT_TILE
    qi = pl.program_id(2)
    ncmp = kc_ref.shape[0]
    row = lax.broadcasted_iota(jnp.int32, (t, ncmp), 0)
    lane = lax.broadcasted_iota(jnp.int32, (t, ncmp), 1)
    q_pos = qi * t + row
    valid = (lane * NSA_CMP_STRIDE + NSA_CMP_BLOCK - 1 <= q_pos) & (lane < ncmp - 1)
    kc = kc_ref[...]
    vc = vc_ref[...].astype(jnp.bfloat16)
    scale = HEAD_DIM ** -0.5
    imp = jnp.zeros((t, ncmp), jnp.float32)
    for j in range(NSA_HPG):
        q = q_ref[:, j * HEAD_DIM:(j + 1) * HEAD_DIM]
        s = lax.dot_general(q, kc, _NT, precision=_HI, preferred_element_type=jnp.float32) * scale
        s = jnp.where(valid, s, -jnp.inf)
        m = jnp.max(s, axis=-1, keepdims=True)
        m = jnp.where(m == -jnp.inf, 0.0, m)
        e = jnp.exp(s - m)
        p = e / jnp.maximum(jnp.sum(e, axis=-1, keepdims=True), 1e-30)
        imp += p
        o_ref[:, j * HEAD_DIM:(j + 1) * HEAD_DIM] = jnp.dot(
            p.astype(jnp.bfloat16), vc, preferred_element_type=jnp.float32).astype(o_ref.dtype)
    ratio = NSA_SEL_BLOCK // NSA_CMP_STRIDE
    front = NSA_CMP_BLOCK // NSA_CMP_STRIDE - 1
    nn = lax.broadcasted_iota(jnp.int32, (ncmp, 128), 0)
    bb = lax.broadcasted_iota(jnp.int32, (ncmp, 128), 1)
    spread = ((nn >= ratio * bb - front) & (nn < ratio * bb + ratio) & (bb < n_sel_blocks)).astype(jnp.float32)
    slc = jnp.dot(imp, spread, precision=_HI, preferred_element_type=jnp.float32)
    row_b = lax.broadcasted_iota(jnp.int32, (t, 128), 0)
    blk = lax.broadcasted_iota(jnp.int32, (t, 128), 1)
    own = (qi * t + row_b) // NSA_SEL_BLOCK
    score = jnp.where((blk == 0) | (blk == own - 1), jnp.inf, slc)
    score = jnp.where(blk < own, score, -jnp.inf)
    sel = _select_top(score, min(NSA_SEL_COUNT - 1, n_sel_blocks - 1))
    sel_ref[...] = jnp.where(blk == own, 1.0, sel)


def cmp_select_prompt(p3, kv_cmp):
    b, t, _ = p3.shape
    nq = t // ATT_TILE
    gw = NSA_HPG * HEAD_DIM
    n_ch = kv_cmp.shape[3]
    return pl.pallas_call(
        functools.partial(_cmp_select_prompt_kernel, n_sel_blocks=t // NSA_SEL_BLOCK),
        grid=(b, NSA_GROUPS, nq),
        in_specs=[pl.BlockSpec((None, ATT_TILE, gw), lambda bi, g, qi: (bi, qi, C_NQ // gw + g)),
                  pl.BlockSpec((None, None, None, n_ch, HEAD_DIM), lambda bi, g, qi: (bi, 0, g, 0, 0)),
                  pl.BlockSpec((None, None, None, n_ch, HEAD_DIM), lambda bi, g, qi: (bi, 1, g, 0, 0))],
        out_specs=[pl.BlockSpec((None, ATT_TILE, gw), lambda bi, g, qi: (bi, qi, g)),
                   pl.BlockSpec((None, None, ATT_TILE, 128), lambda bi, g, qi: (bi, g, qi, 0))],
        out_shape=[jax.ShapeDtypeStruct((b, t, BRANCH_WIDTH), jnp.float32),
                   jax.ShapeDtypeStruct((b, NSA_GROUPS, t, 128), jnp.float32)],
        compiler_params=pltpu.CompilerParams(
            dimension_semantics=("parallel", "parallel", "parallel"), vmem_limit_bytes=VMEM_LIMIT),
    )(p3, kv_cmp, kv_cmp)


def _nsa_prompt_kernel(q_ref, ks_ref, vs_ref, kw_ref, vw_ref, sel_ref, ocmp_ref, ng_ref, bias_ref, o_ref,
                       keep_ref, s_ref, m_ref, l_ref, acc_ref, *, nblk):
    t = ATT_TILE
    g = pl.program_id(1)
    qi = pl.program_id(2)
    r, c = _tile_iotas()
    causal = _causal_tile()
    scale = HEAD_DIM ** -0.5
    per_tile = t // NSA_SEL_BLOCK
    sel = sel_ref[...]
    bb = lax.broadcasted_iota(jnp.int32, (128, t), 0)
    cc = lax.broadcasted_iota(jnp.int32, (128, t), 1)
    for n in range(nblk):
        @pl.when(n <= qi)
        def _(n=n):
            expand = (bb == per_tile * n + cc // NSA_SEL_BLOCK).astype(jnp.float32)
            keep_ref[:, n * t:(n + 1) * t] = jnp.dot(sel, expand, preferred_element_type=jnp.float32)
    gates = jax.nn.sigmoid(ng_ref[...])
    for j in range(NSA_HPG):
        q_bf = q_ref[:, j * HEAD_DIM:(j + 1) * HEAD_DIM].astype(jnp.bfloat16)

        def bias_of(n, j=j):
            return bias_ref[j, jnp.minimum(qi - n, N_BIAS_TILES - 1)]

        def sel_score(n, q_bf=q_bf, bias_of=bias_of):
            s = lax.dot_general(q_bf, ks_ref[n * t:(n + 1) * t, :].astype(jnp.bfloat16), _NT,
                                preferred_element_type=jnp.float32)
            s = s * scale + bias_of(n)
            keep = keep_ref[:, n * t:(n + 1) * t] * jnp.where(n < qi, 1.0, causal)
            return jnp.where(keep > 0.5, s, -jnp.inf)

        o_sel = _attend(nblk, lambda n: n <= qi, sel_score,
                        lambda n: vs_ref[n * t:(n + 1) * t, :].astype(jnp.bfloat16), s_ref, m_ref, l_ref, acc_ref)

        def win_score(n, q_bf=q_bf, bias_of=bias_of):
            s = lax.dot_general(q_bf, kw_ref[n * t:(n + 1) * t, :].astype(jnp.bfloat16), _NT,
                                preferred_element_type=jnp.float32)
            s = s * scale + bias_of(n)
            dist = (qi - n) * t + r - c
            return jnp.where((dist >= 0) & (dist < NSA_WINDOW), s, -jnp.inf)

        back = NSA_WINDOW // t
        o_win = _attend(nblk, lambda n: (n <= qi) & (n >= qi - back), win_score,
                        lambda n: vw_ref[n * t:(n + 1) * t, :].astype(jnp.bfloat16), s_ref, m_ref, l_ref, acc_ref)
        o_cmp = ocmp_ref[:, j * HEAD_DIM:(j + 1) * HEAD_DIM]
        gate_lane = lax.broadcasted_iota(jnp.int32, gates.shape, 1) - 3 * (g * NSA_HPG + j)
        g_cmp = jnp.sum(jnp.where(gate_lane == 0, gates, 0.0), axis=-1, keepdims=True)
        g_sel = jnp.sum(jnp.where(gate_lane == 1, gates, 0.0), axis=-1, keepdims=True)
        g_win = jnp.sum(jnp.where(gate_lane == 2, gates, 0.0), axis=-1, keepdims=True)
        o = g_cmp * o_cmp + g_sel * o_sel + g_win * o_win
        o_ref[:, j * HEAD_DIM:(j + 1) * HEAD_DIM] = o.astype(o_ref.dtype)


def nsa_prompt(p3, sel_mask, o_cmp, bias_tiles):
    b, t, _ = p3.shape
    nblk = t // ATT_TILE
    hd = HEAD_DIM
    gw = NSA_HPG * hd
    g_n = NSA_GROUPS

    def kv_spec(col):
        return pl.BlockSpec((None, t, hd), lambda bi, g, qi: (bi, 0, col // hd + g))

    return pl.pallas_call(
        functools.partial(_nsa_prompt_kernel, nblk=nblk),
        grid=(b, g_n, nblk),
        in_specs=[pl.BlockSpec((None, ATT_TILE, gw), lambda bi, g, qi: (bi, qi, C_NQ // gw + g)),
                  kv_spec(C_SK), kv_spec(C_SV), kv_spec(C_WK), kv_spec(C_WV),
                  pl.BlockSpec((None, None, ATT_TILE, 128), lambda bi, g, qi: (bi, g, qi, 0)),
                  pl.BlockSpec((None, ATT_TILE, gw), lambda bi, g, qi: (bi, qi, g)),
                  pl.BlockSpec((None, ATT_TILE, 128), lambda bi, g, qi: (bi, qi, C_NG // 128)),
                  pl.BlockSpec((None, NSA_HPG, N_BIAS_TILES, ATT_TILE, ATT_TILE),
                               lambda bi, g, qi: (g, 0, 0, 0, 0))],
        out_specs=pl.BlockSpec((None, ATT_TILE, gw), lambda bi, g, qi: (bi, qi, g)),
        out_shape=jax.ShapeDtypeStruct((b, t, BRANCH_WIDTH), jnp.bfloat16),
        scratch_shapes=[pltpu.VMEM((ATT_TILE, t), jnp.float32),
                        pltpu.VMEM((ATT_TILE, t), jnp.float32),
                        pltpu.VMEM((ATT_TILE, 1), jnp.float32),
                        pltpu.VMEM((ATT_TILE, 1), jnp.float32),
                        pltpu.VMEM((ATT_TILE, hd), jnp.float32)],
        compiler_params=pltpu.CompilerParams(
            dimension_semantics=("parallel", "parallel", "parallel"), vmem_limit_bytes=VMEM_LIMIT),
    )(p3, p3, p3, p3, p3, sel_mask, o_cmp, p3, bias_tiles)


def _masked_softmax(s, mask, axis):
    s = jnp.where(mask, s, -jnp.inf)
    m = jnp.max(s, axis=axis, keepdims=True)
    m = jnp.where(jnp.isfinite(m), m, 0.0)
    e = jnp.where(mask, jnp.exp(s - m), 0.0)
    return e / jnp.maximum(jnp.sum(e, axis=axis, keepdims=True), 1e-30)


def _topk_mask(score, n_top):
    nb = score.shape[-1]
    iota = lax.broadcasted_iota(jnp.int32, score.shape, score.ndim - 1)
    sel = jnp.zeros(score.shape, bool)
    for _ in range(n_top):
        m = jnp.max(score, axis=-1, keepdims=True)
        first = jnp.min(jnp.where(score == m, iota, nb), axis=-1, keepdims=True)
        hit = (iota == first) & (m > -jnp.inf)
        sel = sel | hit
        score = jnp.where(iota == first, -jnp.inf, score)
    return sel


def _bias_of_dist(strip, q_pos, k_pos):
    d = jnp.clip(q_pos[:, None] - k_pos[None, :], 0, REL_MAX_DIST)
    return strip[d]


def moba_attention(q, q_start, k, v, strip):
    b, t, h, d = q.shape
    l = k.shape[1]
    nb = -(-l // MOBA_BLOCK)
    pad = nb * MOBA_BLOCK - l
    kp = jnp.pad(k, ((0, 0), (0, pad), (0, 0), (0, 0)))
    k_mean = jnp.mean(kp.reshape(b, nb, MOBA_BLOCK, h, d), axis=2)
    q_pos = q_start + jnp.arange(t)
    k_pos = jnp.arange(l)
    own = q_pos // MOBA_BLOCK
    blk = jnp.arange(nb)
    score = jnp.einsum('bthd,bnhd->bthn', q, k_mean, precision=lax.Precision.HIGHEST)
    score = jnp.where((blk[None, :] < own[:, None])[None, :, None, :], score, -jnp.inf)
    n_top = min(MOBA_TOPK, (q_start + t - 1) // MOBA_BLOCK)
    sel = _topk_mask(score, n_top) | (blk[None, :] == own[:, None])[None, :, None, :]
    kmask = jnp.take(sel, k_pos // MOBA_BLOCK, axis=-1)
    mask = kmask & (q_pos[:, None] >= k_pos[None, :])[None, :, None, :]
    bias = jnp.transpose(_bias_of_dist(strip, q_pos, k_pos), (0, 2, 1))
    s = jnp.einsum('bthd,bshd->bths', q, k) * d ** -0.5 + bias[None]
    p = _masked_softmax(s, mask, -1)
    return jnp.einsum('bths,bshd->bthd', p, v)


def diff_attention(q, q_start, k, v, lam, strip):
    t = q.shape[1]
    l = k.shape[1]
    d = q.shape[-1]
    q_pos = q_start + jnp.arange(t)
    k_pos = jnp.arange(l)
    bias = jnp.transpose(_bias_of_dist(strip, q_pos, k_pos), (2, 0, 1))
    s = jnp.einsum('bthmd,bshmd->bhmts', q, k) * d ** -0.5 + bias[None, :, None]
    p = _masked_softmax(s, (q_pos[:, None] >= k_pos[None, :]), -1)
    a = p[:, :, 0] - lam * p[:, :, 1]
    return jnp.einsum('bhts,bshe->bthe', a, v)


def nsa_compress(k, pos_emb, w1, b1, w2, b2):
    b, l, g, d = k.shape
    r = NSA_CMP_BLOCK // NSA_CMP_STRIDE
    n_ch = l // NSA_CMP_STRIDE
    n_cmp = n_ch - r + 1
    ch = k[:, :n_ch * NSA_CMP_STRIDE].reshape(b, n_ch, NSA_CMP_STRIDE, g, d)
    pe = pos_emb.reshape(r, NSA_CMP_STRIDE, d)
    w1r = w1.reshape(r, NSA_CMP_STRIDE, d, NSA_CMP_HIDDEN)
    pre = b1
    for s in range(r):
        pre = pre + jnp.einsum('bnlgd,ldh->bngh', ch[:, s:s + n_cmp] + pe[s][:, None, :], w1r[s])
    return jnp.einsum('bngh,hd->bngd', jax.nn.gelu(pre), w2) + b2


def nsa_cmp_sel(q, q_start, kc, vc, ks, vs, cmp_pos, cmp_w1, cmp_b1, cmp_w2, cmp_b2, strip):
    b, t, g, j, d = q.shape
    l = kc.shape[1]
    scale = d ** -0.5
    q_pos = q_start + jnp.arange(t)
    k_pos = jnp.arange(l)
    k_cmp = nsa_compress(kc, cmp_pos[0], cmp_w1[0], cmp_b1[0], cmp_w2[0], cmp_b2[0])
    v_cmp = nsa_compress(vc, cmp_pos[1], cmp_w1[1], cmp_b1[1], cmp_w2[1], cmp_b2[1])
    n_cmp = k_cmp.shape[1]
    cmp_end = jnp.arange(n_cmp) * NSA_CMP_STRIDE + NSA_CMP_BLOCK - 1
    valid = cmp_end[None, :] <= q_pos[:, None]
    s = jnp.einsum('btgjd,bngd->bgjtn', q, k_cmp, precision=lax.Precision.HIGHEST) * scale
    p = _masked_softmax(s, valid, -1)
    o_cmp = jnp.einsum('bgjtn,bngd->btgjd', p, v_cmp)
    imp = jnp.sum(p, axis=2)
    nbs = -(-l // NSA_SEL_BLOCK)
    ratio = NSA_SEL_BLOCK // NSA_CMP_STRIDE
    front = NSA_CMP_BLOCK // NSA_CMP_STRIDE - 1
    span = ratio + front
    back = max(ratio * (nbs - 1) + span - front - n_cmp, 0)
    imp = jnp.pad(imp, ((0, 0), (0, 0), (0, 0), (front, back)))
    slc = sum(imp[..., o:o + ratio * nbs:ratio] for o in range(span))
    slc = jnp.transpose(slc, (0, 2, 1, 3))
    own = q_pos // NSA_SEL_BLOCK
    blk = jnp.arange(nbs)
    elig = blk[None, :] < own[:, None]
    forced = (blk[None, :] == 0) | (blk[None, :] == own[:, None] - 1)
    score = jnp.where(forced[None, :, None, :], jnp.inf, slc)
    score = jnp.where(elig[None, :, None, :], score, -jnp.inf)
    n_top = min(NSA_SEL_COUNT - 1, (q_start + t - 1) // NSA_SEL_BLOCK)
    sel = _topk_mask(score, n_top) | (blk[None, :] == own[:, None])[None, :, None, :]
    kmask = jnp.take(sel, k_pos // NSA_SEL_BLOCK, axis=-1)
    mask = kmask & (q_pos[:, None] >= k_pos[None, :])[None, :, None, :]
    mask = jnp.transpose(mask, (0, 2, 1, 3))[:, :, None]
    bias = _bias_of_dist(strip, q_pos, k_pos).reshape(t, l, g, j)
    bias = jnp.transpose(bias, (2, 3, 0, 1))
    s2 = jnp.einsum('btgjd,bsgd->bgjts', q, ks) * scale + bias[None]
    p2 = _masked_softmax(s2, mask, -1)
    o_sel = jnp.einsum('bgjts,bsgd->btgjd', p2, vs)
    return o_cmp, o_sel


def window_attention(q, q_pos, k, v, k_pos, strip):
    b, t, g, j, d = q.shape
    l = k.shape[1]
    dist = q_pos[:, None] - k_pos[None, :]
    mask = (k_pos[None, :] >= 0) & (dist >= 0) & (dist < NSA_WINDOW)
    bias = jnp.transpose(_bias_of_dist(strip, q_pos, k_pos).reshape(t, l, g, j), (2, 3, 0, 1))
    s = jnp.einsum('btgjd,bsgd->bgjts', q, k) * d ** -0.5 + bias[None]
    p = _masked_softmax(s, mask, -1)
    return jnp.einsum('bgjts,bsgd->btgjd', p, v)


def _layer(x, q_start, past, win_buf, conv_buf, bias, lam_init, wts):
    (g_attn, w_in_p, lam_p, subln_g, cmp_pos, cmp_w1, cmp_b1, cmp_w2, cmp_b2,
     w_branch, w_out, g_ffn, w_gate, w_up, w_down, conv_w, conv_b) = wts
    b, t, _ = x.shape
    m = b * t
    hd = HEAD_DIM
    x2 = x.reshape(m, D_MODEL)
    h = rmsnorm_rows(x2, g_attn, RMS_EPS, jnp.bfloat16)
    proj = matmul(h, w_in_p)
    p3 = proj.reshape(b, t, PROJ_PACKED)

    def cols(start, width):
        return p3[:, :, start:start + width]

    moba_rows = cols(C_MK, 2 * _BW).reshape(b, t, 2, MOBA_HEADS, hd)
    diff_rows = cols(C_DK, 2 * _BW).reshape(b, t, 2, DIFF_HEADS, DIFF_VDIM)
    nsa_rows = cols(C_CK, 4 * _GW).reshape(b, t, 4, NSA_GROUPS, hd)
    win_rows = cols(C_WK, 2 * _GW).reshape(b, t, 2, NSA_GROUPS, hd)
    strip, tiles = bias
    n_ab = MOBA_HEADS + DIFF_HEADS
    lam_f = lam_p.astype(jnp.float32)
    lam = jnp.exp(jnp.sum(lam_f[0] * lam_f[1])) - jnp.exp(jnp.sum(lam_f[2] * lam_f[3])) + lam_init
    if past is None:
        o_a = moba_prompt(p3, tiles[:MOBA_HEADS])
        o_b = diff_prompt(p3, tiles[MOBA_HEADS:n_ab], lam, subln_g, lam_init)
        kv_cmp = compress_prompt(p3, cmp_pos, cmp_w1, cmp_b1, cmp_w2, cmp_b2)
        o_cmp, sel_mask = cmp_select_prompt(p3, kv_cmp)
        o_c = nsa_prompt(p3, sel_mask, o_cmp,
                         tiles[n_ab:].reshape(NSA_GROUPS, NSA_HPG, N_BIAS_TILES, ATT_TILE, ATT_TILE))
        new_win = win_rows[:, t - min(NSA_WINDOW, t):]
    else:
        moba_all = jnp.concatenate([past[0], moba_rows], axis=1)
        diff_all = jnp.concatenate([past[1], diff_rows], axis=1)
        nsa_all = jnp.concatenate([past[2], nsa_rows], axis=1)
        l = moba_all.shape[1]
        o_a = moba_attention(cols(C_MQ, _BW).reshape(b, t, MOBA_HEADS, hd), q_start,
                             moba_all[:, :, 0], moba_all[:, :, 1], strip[:, :MOBA_HEADS])
        o_b = diff_attention(cols(C_DQ, _BW).reshape(b, t, DIFF_HEADS, 2, hd), q_start,
                             diff_all[:, :, 0].reshape(b, l, DIFF_HEADS, 2, hd), diff_all[:, :, 1], lam,
                             strip[:, MOBA_HEADS:n_ab])
        o_b = o_b * lax.rsqrt(jnp.mean(o_b * o_b, axis=-1, keepdims=True) + SUBLN_EPS) * subln_g * (1.0 - lam_init)
        strip_c = strip[:, n_ab:]
        nq = cols(C_NQ, _BW).reshape(b, t, NSA_GROUPS, NSA_HPG, hd)
        o_cmp, o_sel = nsa_cmp_sel(nq, q_start, nsa_all[:, :, 0], nsa_all[:, :, 1], nsa_all[:, :, 2],
                                   nsa_all[:, :, 3], cmp_pos, cmp_w1, cmp_b1, cmp_w2, cmp_b2, strip_c)
        w = win_buf.shape[1]
        kv = jnp.concatenate([win_buf, win_rows], axis=1)
        o_win = window_attention(nq, q_start + jnp.arange(t), kv[:, :, 0], kv[:, :, 1],
                                 q_start - w + jnp.arange(w + t), strip_c)
        new_win = kv[:, t:]
        g_nsa = jax.nn.sigmoid(cols(C_NG, N_NG).reshape(b, t, NSA_GROUPS, NSA_HPG, 3))
        o_c = g_nsa[..., 0:1] * o_cmp + g_nsa[..., 1:2] * o_sel + g_nsa[..., 2:3] * o_win
    branches = [o.reshape(m, _BW).astype(jnp.bfloat16) for o in (o_a, o_b, o_c)]
    mix = merge_branches(branches, w_branch, proj)
    x1 = matmul(mix, w_out, x2)

    hf = rmsnorm_rows(x1, g_ffn, RMS_EPS, jnp.bfloat16)
    if conv_buf is None:
        tm = 512
        hprev = hf.reshape(b, t // tm, tm, D_MODEL)[:, :, tm - 8:, :]
        hprev = jnp.concatenate([jnp.zeros_like(hprev[:, :1]), hprev[:, :-1]], axis=1)
        gprev = matmul(hprev.reshape(m // tm * 8, D_MODEL), w_gate).reshape(m // tm, 8, D_FF_PAD)
        act = ffn_in(hf, gprev, w_gate, w_up, conv_w, conv_b, tm=tm)
        g_last = matmul(hf.reshape(b, t, D_MODEL)[:, t - 8:].reshape(b * 8, D_MODEL), w_gate)
        new_conv = g_last.reshape(b, 8, D_FF_PAD)[:, 8 - (CONV_WIDTH - 1):, :D_FF]
    else:
        g = matmul(hf, w_gate).reshape(b, t, D_FF_PAD)
        u = matmul(hf, w_up)
        prev = jnp.pad(conv_buf, ((0, 0), (0, 0), (0, D_FF_PAD - D_FF)))
        gp = jnp.concatenate([prev, g], axis=1)
        shifted = [gp[:, i:i + t].reshape(m, D_FF_PAD) for i in range(CONV_WIDTH)]
        act = conv_act(*shifted, u, conv_w, conv_b)
        new_conv = gp[:, t:, :D_FF]
    x2o = matmul(act, w_down, x1, tm=1024, tn=1024, tk=2816)
    return x2o.reshape(b, t, D_MODEL), moba_rows, diff_rows, nsa_rows, new_win, new_conv


def _pack_w_in(w_in):
    head = w_in[:, :, :C_BG_SRC]
    head = jnp.pad(head, ((0, 0), (0, 0), (0, NG_PAD - N_NG)))
    return jnp.concatenate([head, w_in[:, :, C_BG_SRC:]], axis=2).astype(jnp.bfloat16)


def kernel(x_prompt, x_sample, cache_moba_kv, cache_diff_kv, cache_nsa_kv, state_nsa_win, state_ffn_conv, page_table, rel_bias, norm_attn, w_in, diff_lambda, diff_subln, nsa_cmp_pos, nsa_cmp_w1, nsa_cmp_b1, nsa_cmp_w2, nsa_cmp_b2, w_branch, w_out, norm_ffn, w_gate, w_up, w_down, conv_w, conv_b, norm_final):
    depth = w_in.shape[0]
    past_len = page_table.shape[1] * cache_moba_kv.shape[2]
    bf = jnp.bfloat16
    fpad = D_FF_PAD - D_FF
    w_in_p = _pack_w_in(w_in)
    w_branch_b = w_branch.astype(bf)
    w_out_b = w_out.astype(bf)
    w_gate_b = jnp.pad(w_gate, ((0, 0), (0, 0), (0, fpad))).astype(bf)
    w_up_b = jnp.pad(w_up, ((0, 0), (0, 0), (0, fpad))).astype(bf)
    w_down_b = jnp.pad(w_down, ((0, 0), (0, fpad), (0, 0))).astype(bf)
    conv_w_p = jnp.pad(conv_w, ((0, 0), (0, 0), (0, fpad)))
    conv_b_p = jnp.pad(conv_b, ((0, 0), (0, fpad))).reshape(depth, 1, D_FF_PAD)
    strip = rel_bias[jnp.asarray(_BUCKET)]
    bias = (strip, _bias_tiles(strip))

    y_p, y_s = x_prompt, x_sample
    outs_p = [[] for _ in range(5)]
    outs_s = [[] for _ in range(5)]
    for layer in range(depth):
        lam_init = 0.8 - 0.6 * math.exp(-0.3 * layer)
        wts = (norm_attn[layer], w_in_p[layer], diff_lambda[layer], diff_subln[layer], nsa_cmp_pos[layer],
               nsa_cmp_w1[layer], nsa_cmp_b1[layer], nsa_cmp_w2[layer], nsa_cmp_b2[layer], w_branch_b[layer],
               w_out_b[layer], norm_ffn[layer], w_gate_b[layer], w_up_b[layer], w_down_b[layer],
               conv_w_p[layer], conv_b_p[layer])
        y_p, *new_p = _layer(y_p, 0, None, None, None, bias, lam_init, wts)

        def gather(pool):
            rows = pool[page_table]
            return rows.reshape((rows.shape[0], rows.shape[1] * rows.shape[2]) + rows.shape[3:])

        past = (gather(cache_moba_kv[layer]), gather(cache_diff_kv[layer]), gather(cache_nsa_kv[layer]))
        y_s, *new_s = _layer(y_s, past_len, past, state_nsa_win[layer], state_ffn_conv[layer], bias, lam_init, wts)
        for lst, a in zip(outs_p, new_p):
            lst.append(a)
        for lst, a in zip(outs_s, new_s):
            lst.append(a)
    bp, tp, _ = y_p.shape
    bs, ts, _ = y_s.shape
    y_p = rmsnorm_rows(y_p.reshape(bp * tp, D_MODEL), norm_final, RMS_EPS, jnp.float32).reshape(bp, tp, D_MODEL)
    y_s = rmsnorm_rows(y_s.reshape(bs * ts, D_MODEL), norm_final, RMS_EPS, jnp.float32).reshape(bs, ts, D_MODEL)
    sp = [jnp.stack(l) for l in outs_p]
    ss = [jnp.stack(l) for l in outs_s]
    return (y_p, y_s, sp[0], ss[0], sp[1], ss[1], sp[2], ss[2], sp[3], ss[3], sp[4], ss[4])
```

```python
import functools
import math

import numpy as np
import jax
import jax.numpy as jnp
from jax import lax
from jax.experimental import pallas as pl
from jax.experimental.pallas import tpu as pltpu

D_MODEL = 4096
HEAD_DIM = 128
MOBA_HEADS = 8
MOBA_BLOCK = 256
MOBA_TOPK = 3
DIFF_HEADS = 4
DIFF_VDIM = 2 * HEAD_DIM
NSA_HEADS = 8
NSA_GROUPS = 2
NSA_HPG = NSA_HEADS // NSA_GROUPS
NSA_CMP_BLOCK = 32
NSA_CMP_STRIDE = 16
NSA_CMP_HIDDEN = 2 * HEAD_DIM
NSA_SEL_BLOCK = 64
NSA_SEL_COUNT = 16
NSA_WINDOW = 512
REL_BUCKETS = 32
REL_MAX_DIST = 128
N_BRANCH = 3
BRANCH_WIDTH = MOBA_HEADS * HEAD_DIM
D_FF = 11008
CONV_WIDTH = 3
RMS_EPS = 1e-6
SUBLN_EPS = 1e-5

_BW = BRANCH_WIDTH
_GW = NSA_GROUPS * HEAD_DIM
C_MQ, C_MK, C_MV = 0, _BW, 2 * _BW
C_DQ, C_DK, C_DV = 3 * _BW, 4 * _BW, 5 * _BW
C_NQ = 6 * _BW
C_CK = 7 * _BW
C_CV, C_SK, C_SV, C_WK, C_WV = (C_CK + _GW, C_CK + 2 * _GW, C_CK + 3 * _GW,
                                C_CK + 4 * _GW, C_CK + 5 * _GW)
C_NG = C_CK + 6 * _GW
N_NG = 3 * NSA_HEADS
C_BG_SRC = C_NG + N_NG
NG_PAD = 512
C_BG = C_NG + NG_PAD
PROJ_PACKED = C_BG + N_BRANCH * D_MODEL
D_FF_PAD = 11264

VMEM_LIMIT = 56 * 1024 * 1024


def _bucket_of_dist():
    n = np.arange(REL_MAX_DIST + 1)
    exact = REL_BUCKETS // 2
    nf = np.maximum(n, 1).astype(np.float64)
    large = exact + (np.log(nf / exact) / math.log(REL_MAX_DIST / exact) * (REL_BUCKETS - exact)).astype(np.int64)
    return np.where(n < exact, n, np.minimum(large, REL_BUCKETS - 1)).astype(np.int32)


_BUCKET = _bucket_of_dist()


def _rmsnorm_kernel(x_ref, g_ref, o_ref, *, eps):
    x = x_ref[...]
    ms = jnp.mean(x * x, axis=-1, keepdims=True)
    o_ref[...] = (x * lax.rsqrt(ms + eps) * g_ref[...]).astype(o_ref.dtype)


def rmsnorm_rows(x, g, eps, out_dtype):
    m, d = x.shape
    tm = min(m, 256)
    return pl.pallas_call(
        functools.partial(_rmsnorm_kernel, eps=eps),
        grid=(m // tm,),
        in_specs=[pl.BlockSpec((tm, d), lambda i: (i, 0)),
                  pl.BlockSpec((1, d), lambda i: (0, 0))],
        out_specs=pl.BlockSpec((tm, d), lambda i: (i, 0)),
        out_shape=jax.ShapeDtypeStruct((m, d), out_dtype),
        compiler_params=pltpu.CompilerParams(dimension_semantics=("parallel",)),
    )(x, g.reshape(1, d).astype(jnp.float32))


def _matmul_kernel(a_ref, w_ref, *rest, nk, has_res):
    if has_res:
        r_ref, o_ref = rest
    else:
        (o_ref,) = rest
    part = jnp.dot(a_ref[...], w_ref[...], preferred_element_type=jnp.float32)
    if nk == 1:
        o_ref[...] = part + r_ref[...] if has_res else part
        return
    k = pl.program_id(2)

    @pl.when(k == 0)
    def _():
        o_ref[...] = part + r_ref[...] if has_res else part

    @pl.when(k > 0)
    def _():
        o_ref[...] += part


def matmul(a, w, res=None, *, tm=512, tn=1024, tk=None):
    m, kd = a.shape
    n = w.shape[1]
    tm = min(tm, m)
    tn = min(tn, n)
    tk = kd if tk is None else tk
    nk = kd // tk
    in_specs = [pl.BlockSpec((tm, tk), lambda j, i, k: (i, k)),
                pl.BlockSpec((tk, tn), lambda j, i, k: (k, j))]
    args = [a, w]
    if res is not None:
        in_specs.append(pl.BlockSpec((tm, tn), lambda j, i, k: (i, j)))
        args.append(res)
    return pl.pallas_call(
        functools.partial(_matmul_kernel, nk=nk, has_res=res is not None),
        grid=(n // tn, m // tm, nk),
        in_specs=in_specs,
        out_specs=pl.BlockSpec((tm, tn), lambda j, i, k: (i, j)),
        out_shape=jax.ShapeDtypeStruct((m, n), jnp.float32),
        compiler_params=pltpu.CompilerParams(
            dimension_semantics=("parallel", "parallel", "arbitrary"),
            vmem_limit_bytes=VMEM_LIMIT),
    )(*args)


def _merge_kernel(b0_ref, b1_ref, b2_ref, wb_ref, g0_ref, g1_ref, g2_ref, o_ref):
    acc = None
    for m, (b_ref, g_ref) in enumerate(((b0_ref, g0_ref), (b1_ref, g1_ref), (b2_ref, g2_ref))):
        u = jnp.dot(b_ref[...], wb_ref[m], preferred_element_type=jnp.float32)
        t = jax.nn.sigmoid(g_ref[...]) * u
        acc = t if acc is None else acc + t
    o_ref[...] = acc.astype(o_ref.dtype)


def merge_branches(branches, w_branch, proj, *, tm=512, tn=1024):
    m, bw = branches[0].shape
    d = w_branch.shape[2]
    tm = min(tm, m)
    gate_specs = [
        pl.BlockSpec((tm, tn), functools.partial(
            lambda j, i, base: (i, base + j), base=(C_BG + mm * d) // tn))
        for mm in range(N_BRANCH)]
    branch_spec = pl.BlockSpec((tm, bw), lambda j, i: (i, 0))
    return pl.pallas_call(
        _merge_kernel,
        grid=(d // tn, m // tm),
        in_specs=[branch_spec] * N_BRANCH + [pl.BlockSpec((N_BRANCH, bw, tn), lambda j, i: (0, 0, j))] + gate_specs,
        out_specs=pl.BlockSpec((tm, tn), lambda j, i: (i, j)),
        out_shape=jax.ShapeDtypeStruct((m, d), jnp.bfloat16),
        compiler_params=pltpu.CompilerParams(
            dimension_semantics=("parallel", "parallel"), vmem_limit_bytes=VMEM_LIMIT),
    )(*branches, w_branch, proj, proj, proj)


def _ffn_in_kernel(h_ref, gprev_ref, wg_ref, wu_ref, cw_ref, cb_ref, o_ref):
    h = h_ref[...]
    g = jnp.dot(h, wg_ref[...], preferred_element_type=jnp.float32)
    u = jnp.dot(h, wu_ref[...], preferred_element_type=jnp.float32)
    tm = g.shape[0]
    gcat = jnp.concatenate([gprev_ref[0], g], axis=0)
    cw = cw_ref[...]
    conv = (gcat[6:6 + tm] * cw[0:1] + gcat[7:7 + tm] * cw[1:2] + g * cw[2:3] + cb_ref[...])
    o_ref[...] = (jax.nn.silu(conv) * u).astype(o_ref.dtype)


def ffn_in(h, gprev, wg, wu, conv_w, conv_b, *, tm=512, tf=512):
    m, d = h.shape
    f = wg.shape[1]
    return pl.pallas_call(
        _ffn_in_kernel,
        grid=(f // tf, m // tm),
        in_specs=[pl.BlockSpec((tm, d), lambda j, i: (i, 0)),
                  pl.BlockSpec((1, 8, tf), lambda j, i: (i, 0, j)),
                  pl.BlockSpec((d, tf), lambda j, i: (0, j)),
                  pl.BlockSpec((d, tf), lambda j, i: (0, j)),
                  pl.BlockSpec((CONV_WIDTH, tf), lambda j, i: (0, j)),
                  pl.BlockSpec((1, tf), lambda j, i: (0, j))],
        out_specs=pl.BlockSpec((tm, tf), lambda j, i: (i, j)),
        out_shape=jax.ShapeDtypeStruct((m, f), jnp.bfloat16),
        compiler_params=pltpu.CompilerParams(
            dimension_semantics=("parallel", "parallel"), vmem_limit_bytes=VMEM_LIMIT),
    )(h, gprev, wg, wu, conv_w, conv_b)


def _conv_act_kernel(g0_ref, g1_ref, g2_ref, u_ref, cw_ref, cb_ref, o_ref):
    cw = cw_ref[...]
    conv = g0_ref[...] * cw[0:1] + g1_ref[...] * cw[1:2] + g2_ref[...] * cw[2:3] + cb_ref[...]
    o_ref[...] = (jax.nn.silu(conv) * u_ref[...]).astype(o_ref.dtype)


def conv_act(g0, g1, g2, u, conv_w, conv_b, *, tf=1024):
    m, f = u.shape
    row = pl.BlockSpec((m, tf), lambda j: (0, j))
    return pl.pallas_call(
        _conv_act_kernel,
        grid=(f // tf,),
        in_specs=[row, row, row, row,
                  pl.BlockSpec((CONV_WIDTH, tf), lambda j: (0, j)),
                  pl.BlockSpec((1, tf), lambda j: (0, j))],
        out_specs=row,
        out_shape=jax.ShapeDtypeStruct((m, f), jnp.bfloat16),
        compiler_params=pltpu.CompilerParams(dimension_semantics=("parallel",)),
    )(g0, g1, g2, u, conv_w, conv_b)


ATT_TILE = 256
N_BIAS_TILES = 3
_HI = lax.Precision.HIGHEST
_NT = (((1,), (1,)), ((), ()))


def _bias_tiles(strip):
    r = np.arange(ATT_TILE)
    d = np.stack([np.clip(o * ATT_TILE + r[:, None] - r[None, :], 0, REL_MAX_DIST) for o in range(N_BIAS_TILES)])
    return jnp.transpose(strip[jnp.asarray(d)], (3, 0, 1, 2))


def _tile_iotas():
    r = lax.broadcasted_iota(jnp.int32, (ATT_TILE, ATT_TILE), 0)
    c = lax.broadcasted_iota(jnp.int32, (ATT_TILE, ATT_TILE), 1)
    return r, c


def _causal_tile():
    r, c = _tile_iotas()
    return (r >= c).astype(jnp.float32)


def _attend(nblk, cond, score_fn, v_fn, s_ref, m_ref, l_ref, acc_ref):
    t = ATT_TILE
    m_ref[...] = jnp.full(m_ref.shape, -jnp.inf, jnp.float32)
    for n in range(nblk):
        @pl.when(cond(n))
        def _(n=n):
            s = score_fn(n)
            s_ref[:, n * t:(n + 1) * t] = s
            m_ref[...] = jnp.maximum(m_ref[...], jnp.max(s, axis=-1, keepdims=True))
    m = m_ref[...]
    m = jnp.where(m == -jnp.inf, 0.0, m)
    l_ref[...] = jnp.zeros(l_ref.shape, jnp.float32)
    acc_ref[...] = jnp.zeros(acc_ref.shape, jnp.float32)
    for n in range(nblk):
        @pl.when(cond(n))
        def _(n=n):
            e = jnp.exp(s_ref[:, n * t:(n + 1) * t] - m)
            l_ref[...] += jnp.sum(e, axis=-1, keepdims=True)
            acc_ref[...] += jnp.dot(e.astype(jnp.bfloat16), v_fn(n), preferred_element_type=jnp.float32)
    return acc_ref[...] / jnp.maximum(l_ref[...], 1e-30)


def _select_top(score, n_top):
    lane = lax.broadcasted_iota(jnp.int32, score.shape, 1)
    sel = jnp.zeros(score.shape, jnp.float32)
    for _ in range(n_top):
        m = jnp.max(score, axis=-1, keepdims=True)
        first = jnp.min(jnp.where(score == m, lane, score.shape[1]), axis=-1, keepdims=True)
        hit = lane == first
        sel = jnp.where(hit & (m > -jnp.inf), 1.0, sel)
        score = jnp.where(hit, -jnp.inf, score)
    return sel


def _moba_prompt_kernel(q_ref, k_ref, v_ref, bias_ref, o_ref, kmean_ref, s_ref, m_ref, l_ref, acc_ref, *, nblk):
    t = ATT_TILE
    qi = pl.program_id(2)

    @pl.when(qi == 0)
    def _():
        kmean_ref[...] = jnp.zeros(kmean_ref.shape, jnp.float32)
        for n in range(nblk):
            kmean_ref[n:n + 1, :] = jnp.mean(k_ref[n * t:(n + 1) * t, :], axis=0, keepdims=True)

    q = q_ref[...]
    lane = lax.broadcasted_iota(jnp.int32, (t, HEAD_DIM), 1)
    score = lax.dot_general(q, kmean_ref[...], _NT, precision=_HI, preferred_element_type=jnp.float32)
    score = jnp.where(lane < qi, score, -jnp.inf)
    sel = _select_top(score, min(MOBA_TOPK, nblk - 1))
    q_bf = q.astype(jnp.bfloat16)
    causal = _causal_tile()
    scale = HEAD_DIM ** -0.5

    def score_fn(n):
        s = lax.dot_general(q_bf, k_ref[n * t:(n + 1) * t, :].astype(jnp.bfloat16), _NT,
                            preferred_element_type=jnp.float32)
        s = s * scale + bias_ref[jnp.minimum(qi - n, N_BIAS_TILES - 1)]
        keep = jnp.where(n == qi, causal, jnp.broadcast_to(sel[:, n:n + 1], (t, t)))
        return jnp.where(keep > 0.5, s, -jnp.inf)

    o = _attend(nblk, lambda n: n <= qi, score_fn,
                lambda n: v_ref[n * t:(n + 1) * t, :].astype(jnp.bfloat16), s_ref, m_ref, l_ref, acc_ref)
    o_ref[...] = o.astype(o_ref.dtype)


def moba_prompt(p3, bias_tiles):
    b, t, _ = p3.shape
    nblk = t // ATT_TILE
    hd = HEAD_DIM
    return pl.pallas_call(
        functools.partial(_moba_prompt_kernel, nblk=nblk),
        grid=(b, MOBA_HEADS, nblk),
        in_specs=[pl.BlockSpec((None, ATT_TILE, hd), lambda bi, h, qi: (bi, qi, C_MQ // hd + h)),
                  pl.BlockSpec((None, t, hd), lambda bi, h, qi: (bi, 0, C_MK // hd + h)),
                  pl.BlockSpec((None, t, hd), lambda bi, h, qi: (bi, 0, C_MV // hd + h)),
                  pl.BlockSpec((None, N_BIAS_TILES, ATT_TILE, ATT_TILE), lambda bi, h, qi: (h, 0, 0, 0))],
        out_specs=pl.BlockSpec((None, ATT_TILE, hd), lambda bi, h, qi: (bi, qi, h)),
        out_shape=jax.ShapeDtypeStruct((b, t, BRANCH_WIDTH), jnp.bfloat16),
        scratch_shapes=[pltpu.VMEM((hd, hd), jnp.float32),
                        pltpu.VMEM((ATT_TILE, t), jnp.float32),
                        pltpu.VMEM((ATT_TILE, 1), jnp.float32),
                        pltpu.VMEM((ATT_TILE, 1), jnp.float32),
                        pltpu.VMEM((ATT_TILE, hd), jnp.float32)],
        compiler_params=pltpu.CompilerParams(
            dimension_semantics=("parallel", "parallel", "arbitrary"), vmem_limit_bytes=VMEM_LIMIT),
    )(p3, p3, p3, bias_tiles)


def _diff_prompt_kernel(lam_ref, q_ref, k_ref, v_ref, bias_ref, g_ref, o_ref, s_ref, m_ref, l_ref, acc_ref,
                        *, nblk, out_scale):
    t = ATT_TILE
    qi = pl.program_id(2)
    causal = _causal_tile()
    scale = HEAD_DIM ** -0.5
    outs = []
    for mp in range(2):
        q_bf = q_ref[:, mp * HEAD_DIM:(mp + 1) * HEAD_DIM].astype(jnp.bfloat16)

        def score_fn(n, mp=mp, q_bf=q_bf):
            kb = k_ref[n * t:(n + 1) * t, mp * HEAD_DIM:(mp + 1) * HEAD_DIM].astype(jnp.bfloat16)
            s = lax.dot_general(q_bf, kb, _NT, preferred_element_type=jnp.float32)
            s = s * scale + bias_ref[jnp.minimum(qi - n, N_BIAS_TILES - 1)]
            return jnp.where(jnp.where(n < qi, 1.0, causal) > 0.5, s, -jnp.inf)

        outs.append(_attend(nblk, lambda n: n <= qi, score_fn,
                            lambda n: v_ref[n * t:(n + 1) * t, :].astype(jnp.bfloat16),
                            s_ref, m_ref, l_ref, acc_ref))
    o = outs[0] - lam_ref[0] * outs[1]
    o = o * lax.rsqrt(jnp.mean(o * o, axis=-1, keepdims=True) + SUBLN_EPS) * g_ref[...] * out_scale
    o_ref[...] = o.astype(o_ref.dtype)


def diff_prompt(p3, bias_tiles, lam, subln_g, lam_init):
    b, t, _ = p3.shape
    nblk = t // ATT_TILE
    w = DIFF_VDIM
    return pl.pallas_call(
        functools.partial(_diff_prompt_kernel, nblk=nblk, out_scale=1.0 - lam_init),
        grid=(b, DIFF_HEADS, nblk),
        in_specs=[pl.BlockSpec(memory_space=pltpu.SMEM),
                  pl.BlockSpec((None, ATT_TILE, w), lambda bi, h, qi: (bi, qi, C_DQ // w + h)),
                  pl.BlockSpec((None, t, w), lambda bi, h, qi: (bi, 0, C_DK // w + h)),
                  pl.BlockSpec((None, t, w), lambda bi, h, qi: (bi, 0, C_DV // w + h)),
                  pl.BlockSpec((None, N_BIAS_TILES, ATT_TILE, ATT_TILE), lambda bi, h, qi: (h, 0, 0, 0)),
                  pl.BlockSpec((1, w), lambda bi, h, qi: (0, 0))],
        out_specs=pl.BlockSpec((None, ATT_TILE, w), lambda bi, h, qi: (bi, qi, h)),
        out_shape=jax.ShapeDtypeStruct((b, t, BRANCH_WIDTH), jnp.bfloat16),
        scratch_shapes=[pltpu.VMEM((ATT_TILE, t), jnp.float32),
                        pltpu.VMEM((ATT_TILE, 1), jnp.float32),
                        pltpu.VMEM((ATT_TILE, 1), jnp.float32),
                        pltpu.VMEM((ATT_TILE, w), jnp.float32)],
        compiler_params=pltpu.CompilerParams(
            dimension_semantics=("parallel", "parallel", "parallel"), vmem_limit_bytes=VMEM_LIMIT),
    )(lam.reshape(1).astype(jnp.float32), p3, p3, p3, bias_tiles, subln_g.reshape(1, w).astype(jnp.float32))


def _gelu_tanh(x):
    return 0.5 * x * (1.0 + jnp.tanh(math.sqrt(2.0 / math.pi) * (x + 0.044715 * (x * x * x))))


def _compress_prompt_kernel(x_ref, pe_ref, w1_ref, b1_ref, w2_ref, b2_ref, o_ref, *, n_ch):
    st = NSA_CMP_STRIDE
    pre_a = jnp.zeros((n_ch, NSA_CMP_HIDDEN), jnp.float32)
    pre_b = jnp.zeros((n_ch, NSA_CMP_HIDDEN), jnp.float32)
    for l in range(st):
        x_l = x_ref[pl.ds(l, n_ch, stride=st), :]
        pre_a += jnp.dot((x_l + pe_ref[l:l + 1, :]).astype(jnp.bfloat16), w1_ref[l],
                         preferred_element_type=jnp.float32)
        pre_b += jnp.dot((x_l + pe_ref[st + l:st + l + 1, :]).astype(jnp.bfloat16), w1_ref[st + l],
                         preferred_element_type=jnp.float32)
    pre = pre_a + pltpu.roll(pre_b, n_ch - 1, 0) + b1_ref[...]
    o_ref[...] = jnp.dot(_gelu_tanh(pre).astype(jnp.bfloat16), w2_ref[...],
                         preferred_element_type=jnp.float32) + b2_ref[...]


def compress_prompt(p3, cmp_pos, cmp_w1, cmp_b1, cmp_w2, cmp_b2):
    b, t, _ = p3.shape
    n_ch = t // NSA_CMP_STRIDE
    hd = HEAD_DIM
    g_n = NSA_GROUPS
    return pl.pallas_call(
        functools.partial(_compress_prompt_kernel, n_ch=n_ch),
        grid=(b, 2, g_n),
        in_specs=[pl.BlockSpec((None, t, hd), lambda bi, kv, g: (bi, 0, C_CK // hd + kv * g_n + g)),
                  pl.BlockSpec((None, NSA_CMP_BLOCK, hd), lambda bi, kv, g: (kv, 0, 0)),
                  pl.BlockSpec((None, NSA_CMP_BLOCK, hd, NSA_CMP_HIDDEN), lambda bi, kv, g: (kv, 0, 0, 0)),
                  pl.BlockSpec((None, 1, NSA_CMP_HIDDEN), lambda bi, kv, g: (kv, 0, 0)),
                  pl.BlockSpec((None, NSA_CMP_HIDDEN, hd), lambda bi, kv, g: (kv, 0, 0)),
                  pl.BlockSpec((None, 1, hd), lambda bi, kv, g: (kv, 0, 0))],
        out_specs=pl.BlockSpec((None, None, None, n_ch, hd), lambda bi, kv, g: (bi, kv, g, 0, 0)),
        out_shape=jax.ShapeDtypeStruct((b, 2, g_n, n_ch, hd), jnp.float32),
        compiler_params=pltpu.CompilerParams(
            dimension_semantics=("parallel", "parallel", "parallel"), vmem_limit_bytes=VMEM_LIMIT),
    )(p3, cmp_pos, cmp_w1.astype(jnp.bfloat16), cmp_b1.reshape(2, 1, NSA_CMP_HIDDEN),
      cmp_w2.astype(jnp.bfloat16), cmp_b2.reshape(2, 1, hd))


def _cmp_select_prompt_kernel(q_ref, kc_ref, vc_ref, o_ref, sel_ref, *, n_sel_blocks):
    t = ATT_TILE
    qi = pl.program_id(2)
    ncmp = kc_ref.shape[0]
    row = lax.broadcasted_iota(jnp.int32, (t, ncmp), 0)
    lane = lax.broadcasted_iota(jnp.int32, (t, ncmp), 1)
    q_pos = qi * t + row
    valid = (lane * NSA_CMP_STRIDE + NSA_CMP_BLOCK - 1 <= q_pos) & (lane < ncmp - 1)
    kc = kc_ref[...]
    vc = vc_ref[...].astype(jnp.bfloat16)
    scale = HEAD_DIM ** -0.5
    imp = jnp.zeros((t, ncmp), jnp.float32)
    for j in range(NSA_HPG):
        q = q_ref[:, j * HEAD_DIM:(j + 1) * HEAD_DIM]
        s = lax.dot_general(q, kc, _NT, precision=_HI, preferred_element_type=jnp.float32) * scale
        s = jnp.where(valid, s, -jnp.inf)
        m = jnp.max(s, axis=-1, keepdims=True)
        m = jnp.where(m == -jnp.inf, 0.0, m)
        e = jnp.exp(s - m)
        p = e / jnp.maximum(jnp.sum(e, axis=-1, keepdims=True), 1e-30)
        imp += p
        o_ref[:, j * HEAD_DIM:(j + 1) * HEAD_DIM] = jnp.dot(
            p.astype(jnp.bfloat16), vc, preferred_element_type=jnp.float32).astype(o_ref.dtype)
    ratio = NSA_SEL_BLOCK // NSA_CMP_STRIDE
    front = NSA_CMP_BLOCK // NSA_CMP_STRIDE - 1
    nn = lax.broadcasted_iota(jnp.int32, (ncmp, 128), 0)
    bb = lax.broadcasted_iota(jnp.int32, (ncmp, 128), 1)
    spread = ((nn >= ratio * bb - front) & (nn < ratio * bb + ratio) & (bb < n_sel_blocks)).astype(jnp.float32)
    slc = jnp.dot(imp, spread, precision=_HI, preferred_element_type=jnp.float32)
    row_b = lax.broadcasted_iota(jnp.int32, (t, 128), 0)
    blk = lax.broadcasted_iota(jnp.int32, (t, 128), 1)
    own = (qi * t + row_b) // NSA_SEL_BLOCK
    score = jnp.where((blk == 0) | (blk == own - 1), jnp.inf, slc)
    score = jnp.where(blk < own, score, -jnp.inf)
    sel = _select_top(score, min(NSA_SEL_COUNT - 1, n_sel_blocks - 1))
    sel_ref[...] = jnp.where(blk == own, 1.0, sel)


def cmp_select_prompt(p3, kv_cmp):
    b, t, _ = p3.shape
    nq = t // ATT_TILE
    gw = NSA_HPG * HEAD_DIM
    n_ch = kv_cmp.shape[3]
    return pl.pallas_call(
        functools.partial(_cmp_select_prompt_kernel, n_sel_blocks=t // NSA_SEL_BLOCK),
        grid=(b, NSA_GROUPS, nq),
        in_specs=[pl.BlockSpec((None, ATT_TILE, gw), lambda bi, g, qi: (bi, qi, C_NQ // gw + g)),
                  pl.BlockSpec((None, None, None, n_ch, HEAD_DIM), lambda bi, g, qi: (bi, 0, g, 0, 0)),
                  pl.BlockSpec((None, None, None, n_ch, HEAD_DIM), lambda bi, g, qi: (bi, 1, g, 0, 0))],
        out_specs=[pl.BlockSpec((None, ATT_TILE, gw), lambda bi, g, qi: (bi, qi, g)),
                   pl.BlockSpec((None, None, ATT_TILE, 128), lambda bi, g, qi: (bi, g, qi, 0))],
        out_shape=[jax.ShapeDtypeStruct((b, t, BRANCH_WIDTH), jnp.float32),
                   jax.ShapeDtypeStruct((b, NSA_GROUPS, t, 128), jnp.float32)],
        compiler_params=pltpu.CompilerParams(
            dimension_semantics=("parallel", "parallel", "parallel"), vmem_limit_bytes=VMEM_LIMIT),
    )(p3, kv_cmp, kv_cmp)


def _nsa_prompt_kernel(q_ref, ks_ref, vs_ref, kw_ref, vw_ref, sel_ref, ocmp_ref, ng_ref, bias_ref, o_ref,
                       keep_ref, s_ref, m_ref, l_ref, acc_ref, *, nblk):
    t = ATT_TILE
    g = pl.program_id(1)
    qi = pl.program_id(2)
    r, c = _tile_iotas()
    causal = _causal_tile()
    scale = HEAD_DIM ** -0.5
    per_tile = t // NSA_SEL_BLOCK
    sel = sel_ref[...]
    bb = lax.broadcasted_iota(jnp.int32, (128, t), 0)
    cc = lax.broadcasted_iota(jnp.int32, (128, t), 1)
    for n in range(nblk):
        @pl.when(n <= qi)
        def _(n=n):
            expand = (bb == per_tile * n + cc // NSA_SEL_BLOCK).astype(jnp.float32)
            keep_ref[:, n * t:(n + 1) * t] = jnp.dot(sel, expand, preferred_element_type=jnp.float32)
    gates = jax.nn.sigmoid(ng_ref[...])
    for j in range(NSA_HPG):
        q_bf = q_ref[:, j * HEAD_DIM:(j + 1) * HEAD_DIM].astype(jnp.bfloat16)

        def bias_of(n, j=j):
            return bias_ref[j, jnp.minimum(qi - n, N_BIAS_TILES - 1)]

        def sel_score(n, q_bf=q_bf, bias_of=bias_of):
            s = lax.dot_general(q_bf, ks_ref[n * t:(n + 1) * t, :].astype(jnp.bfloat16), _NT,
                                preferred_element_type=jnp.float32)
            s = s * scale + bias_of(n)
            keep = keep_ref[:, n * t:(n + 1) * t] * jnp.where(n < qi, 1.0, causal)
            return jnp.where(keep > 0.5, s, -jnp.inf)

        o_sel = _attend(nblk, lambda n: n <= qi, sel_score,
                        lambda n: vs_ref[n * t:(n + 1) * t, :].astype(jnp.bfloat16), s_ref, m_ref, l_ref, acc_ref)

        def win_score(n, q_bf=q_bf, bias_of=bias_of):
            s = lax.dot_general(q_bf, kw_ref[n * t:(n + 1) * t, :].astype(jnp.bfloat16), _NT,
                                preferred_element_type=jnp.float32)
            s = s * scale + bias_of(n)
            dist = (qi - n) * t + r - c
            return jnp.where((dist >= 0) & (dist < NSA_WINDOW), s, -jnp.inf)

        back = NSA_WINDOW // t
        o_win = _attend(nblk, lambda n: (n <= qi) & (n >= qi - back), win_score,
                        lambda n: vw_ref[n * t:(n + 1) * t, :].astype(jnp.bfloat16), s_ref, m_ref, l_ref, acc_ref)
        o_cmp = ocmp_ref[:, j * HEAD_DIM:(j + 1) * HEAD_DIM]
        gate_lane = lax.broadcasted_iota(jnp.int32, gates.shape, 1) - 3 * (g * NSA_HPG + j)
        g_cmp = jnp.sum(jnp.where(gate_lane == 0, gates, 0.0), axis=-1, keepdims=True)
        g_sel = jnp.sum(jnp.where(gate_lane == 1, gates, 0.0), axis=-1, keepdims=True)
        g_win = jnp.sum(jnp.where(gate_lane == 2, gates, 0.0), axis=-1, keepdims=True)
        o = g_cmp * o_cmp + g_sel * o_sel + g_win * o_win
        o_ref[:, j * HEAD_DIM:(j + 1) * HEAD_DIM] = o.astype(o_ref.dtype)


def nsa_prompt(p3, sel_mask, o_cmp, bias_tiles):
    b, t, _ = p3.shape
    nblk = t // ATT_TILE
    hd = HEAD_DIM
    gw = NSA_HPG * hd
    g_n = NSA_GROUPS

    def kv_spec(col):
        return pl.BlockSpec((None, t, hd), lambda bi, g, qi: (bi, 0, col // hd + g))

    return pl.pallas_call(
        functools.partial(_nsa_prompt_kernel, nblk=nblk),
        grid=(b, g_n, nblk),
        in_specs=[pl.BlockSpec((None, ATT_TILE, gw), lambda bi, g, qi: (bi, qi, C_NQ // gw + g)),
                  kv_spec(C_SK), kv_spec(C_SV), kv_spec(C_WK), kv_spec(C_WV),
                  pl.BlockSpec((None, None, ATT_TILE, 128), lambda bi, g, qi: (bi, g, qi, 0)),
                  pl.BlockSpec((None, ATT_TILE, gw), lambda bi, g, qi: (bi, qi, g)),
                  pl.BlockSpec((None, ATT_TILE, 128), lambda bi, g, qi: (bi, qi, C_NG // 128)),
                  pl.BlockSpec((None, NSA_HPG, N_BIAS_TILES, ATT_TILE, ATT_TILE),
                               lambda bi, g, qi: (g, 0, 0, 0, 0))],
        out_specs=pl.BlockSpec((None, ATT_TILE, gw), lambda bi, g, qi: (bi, qi, g)),
        out_shape=jax.ShapeDtypeStruct((b, t, BRANCH_WIDTH), jnp.bfloat16),
        scratch_shapes=[pltpu.VMEM((ATT_TILE, t), jnp.float32),
                        pltpu.VMEM((ATT_TILE, t), jnp.float32),
                        pltpu.VMEM((ATT_TILE, 1), jnp.float32),
                        pltpu.VMEM((ATT_TILE, 1), jnp.float32),
                        pltpu.VMEM((ATT_TILE, hd), jnp.float32)],
        compiler_params=pltpu.CompilerParams(
            dimension_semantics=("parallel", "parallel", "parallel"), vmem_limit_bytes=VMEM_LIMIT),
    )(p3, p3, p3, p3, p3, sel_mask, o_cmp, p3, bias_tiles)


def _page_softmax(s):
    m = jnp.max(s, axis=-1, keepdims=True)
    e = jnp.exp(s - jnp.where(m == -jnp.inf, 0.0, m))
    return m, e, jnp.sum(e, axis=-1, keepdims=True)


def _is_last(p, last_page):
    return (p == last_page).astype(jnp.int32)


def _moba_pages_kernel(pt_ref, q_ref, kv_ref, bias_ref, o_ref, m_ref, l_ref, ks_ref, *, last_page):
    p = pl.program_id(1)
    k3 = kv_ref[:, 0]
    v3 = kv_ref[:, 1]
    rows = k3.shape[0] * k3.shape[1]
    ks_ref[...] = jnp.sum(k3, axis=0)
    kx = k3.reshape(rows, HEAD_DIM).astype(jnp.bfloat16)
    vx = v3.reshape(rows, HEAD_DIM).astype(jnp.bfloat16)
    s = lax.dot_general(q_ref[...], kx, _NT, preferred_element_type=jnp.float32)
    s = s * HEAD_DIM ** -0.5 + bias_ref[_is_last(p, last_page)]
    row = lax.broadcasted_iota(jnp.int32, s.shape, 0)
    lane = lax.broadcasted_iota(jnp.int32, s.shape, 1)
    s = jnp.where((lane & (MOBA_HEADS - 1)) == (row & (MOBA_HEADS - 1)), s, -jnp.inf)
    m, e, l = _page_softmax(s)
    o_ref[...] = jnp.dot(e.astype(jnp.bfloat16), vx, preferred_element_type=jnp.float32)
    m_ref[...] = m
    l_ref[...] = l


def _page_stat_specs(n_rows, width):
    def im(bi, p, pt):
        return (bi, p, 0, 0)
    return [pl.BlockSpec((None, None, n_rows, width), im),
            pl.BlockSpec((None, None, n_rows, 1), im),
            pl.BlockSpec((None, None, n_rows, 1), im)]


def _page_stat_shapes(b, pages, n_rows, width):
    return [jax.ShapeDtypeStruct((b, pages, n_rows, width), jnp.float32),
            jax.ShapeDtypeStruct((b, pages, n_rows, 1), jnp.float32),
            jax.ShapeDtypeStruct((b, pages, n_rows, 1), jnp.float32)]


_PAGE_PARAMS = pltpu.CompilerParams(dimension_semantics=("parallel", "parallel"), vmem_limit_bytes=VMEM_LIMIT)


def moba_pages(q_rows, cache, layer, page_table, bias2):
    b, n_rows, _ = q_rows.shape
    pages = page_table.shape[1]
    page = cache.shape[2]
    assert MOBA_HEADS & (MOBA_HEADS - 1) == 0
    grid_spec = pltpu.PrefetchScalarGridSpec(
        num_scalar_prefetch=1, grid=(b, pages),
        in_specs=[pl.BlockSpec((None, n_rows, HEAD_DIM), lambda bi, p, pt: (bi, 0, 0)),
                  pl.BlockSpec((None, None, page, 2, MOBA_HEADS, HEAD_DIM),
                               lambda bi, p, pt: (layer, pt[bi, p], 0, 0, 0, 0)),
                  pl.BlockSpec((2, n_rows, page * MOBA_HEADS), lambda bi, p, pt: (0, 0, 0))],
        out_specs=_page_stat_specs(n_rows, HEAD_DIM) + [
            pl.BlockSpec((None, None, MOBA_HEADS, HEAD_DIM), lambda bi, p, pt: (bi, p, 0, 0))])
    return pl.pallas_call(
        functools.partial(_moba_pages_kernel, last_page=pages - 1),
        grid_spec=grid_spec,
        out_shape=_page_stat_shapes(b, pages, n_rows, HEAD_DIM) + [
            jax.ShapeDtypeStruct((b, pages, MOBA_HEADS, HEAD_DIM), jnp.float32)],
        compiler_params=_PAGE_PARAMS,
    )(page_table, q_rows, cache, bias2)


def _diff_pages_kernel(pt_ref, q_ref, kv_ref, bias_ref, o_ref, m_ref, l_ref, *, last_page):
    p = pl.program_id(1)
    half = kv_ref.shape[1] // 2
    s = lax.dot_general(q_ref[...], kv_ref[:, :half], _NT, preferred_element_type=jnp.float32)
    s = s * HEAD_DIM ** -0.5 + bias_ref[_is_last(p, last_page)]
    m, e, l = _page_softmax(s)
    pv = jnp.dot(e.astype(jnp.bfloat16), kv_ref[:, half:], preferred_element_type=jnp.float32)
    rph = s.shape[0] // DIFF_HEADS
    for h in range(DIFF_HEADS):
        o_ref[h * rph:(h + 1) * rph, :] = pv[h * rph:(h + 1) * rph, h * DIFF_VDIM:(h + 1) * DIFF_VDIM]
    m_ref[...] = m
    l_ref[...] = l


def diff_pages(q_blk, cache2d, layer, page_table, bias2):
    b, n_rows, kw = q_blk.shape
    pages = page_table.shape[1]
    page = cache2d.shape[2]
    grid_spec = pltpu.PrefetchScalarGridSpec(
        num_scalar_prefetch=1, grid=(b, pages),
        in_specs=[pl.BlockSpec((None, n_rows, kw), lambda bi, p, pt: (bi, 0, 0)),
                  pl.BlockSpec((None, None, page, 2 * kw), lambda bi, p, pt: (layer, pt[bi, p], 0, 0)),
                  pl.BlockSpec((2, n_rows, page), lambda bi, p, pt: (0, 0, 0))],
        out_specs=_page_stat_specs(n_rows, DIFF_VDIM))
    return pl.pallas_call(
        functools.partial(_diff_pages_kernel, last_page=pages - 1),
        grid_spec=grid_spec,
        out_shape=_page_stat_shapes(b, pages, n_rows, DIFF_VDIM),
        compiler_params=_PAGE_PARAMS,
    )(page_table, q_blk, cache2d, bias2)


NSA_ROWS_PER_TOKEN = 4 * NSA_GROUPS


def _compress_pages_kernel(pt_ref, pg0_ref, pg1_ref, pe_ref, w1_ref, b1_ref, w2_ref, b2_ref, o_ref, cbuf_ref,
                           *, n_steps):
    i = pl.program_id(1)
    st = NSA_CMP_STRIDE
    rpt = NSA_ROWS_PER_TOKEN
    chunks = pg0_ref.shape[0] // (st * rpt)
    base = pl.multiple_of(i * 2 * chunks, 2 * chunks)
    for kg in range(2 * NSA_GROUPS):
        for l in range(st):
            x0 = pg0_ref[pl.ds(l * rpt + kg, chunks, stride=st * rpt), :]
            x1 = pg1_ref[pl.ds(l * rpt + kg, chunks, stride=st * rpt), :]
            cbuf_ref[kg, pl.ds(base, 2 * chunks), l * HEAD_DIM:(l + 1) * HEAD_DIM] = (
                jnp.concatenate([x0, x1], axis=0).astype(jnp.bfloat16))

    @pl.when(i == n_steps - 1)
    def _():
        n_ch = cbuf_ref.shape[1]
        half = st * HEAD_DIM
        for kind in range(2):
            const = jnp.dot(pe_ref[kind].astype(jnp.bfloat16), w1_ref[kind],
                            preferred_element_type=jnp.float32)[0:1] + b1_ref[kind]
            for g in range(NSA_GROUPS):
                c = cbuf_ref[kind * NSA_GROUPS + g]
                pre_a = jnp.dot(c, w1_ref[kind, :half, :], preferred_element_type=jnp.float32)
                pre_b = jnp.dot(c, w1_ref[kind, half:, :], preferred_element_type=jnp.float32)
                pre = pre_a + pltpu.roll(pre_b, n_ch - 1, 0) + const
                o_ref[kind, g] = jnp.dot(_gelu_tanh(pre).astype(jnp.bfloat16), w2_ref[kind],
                                         preferred_element_type=jnp.float32) + b2_ref[kind]


def compress_pages(cache2d, layer, page_table, cmp_pos, cmp_w1, cmp_b1, cmp_w2, cmp_b2):
    b, pages = page_table.shape
    rows = cache2d.shape[2]
    page = rows // NSA_ROWS_PER_TOKEN
    n_ch = pages * page // NSA_CMP_STRIDE
    feat = NSA_CMP_BLOCK * HEAD_DIM
    assert pages % 2 == 0 and (2 * page // NSA_CMP_STRIDE) % 16 == 0
    pe = jnp.broadcast_to(cmp_pos.reshape(2, 1, feat), (2, 8, feat))
    grid_spec = pltpu.PrefetchScalarGridSpec(
        num_scalar_prefetch=1, grid=(b, pages // 2),
        in_specs=[pl.BlockSpec((None, None, rows, HEAD_DIM), lambda bi, i, pt: (layer, pt[bi, 2 * i], 0, 0)),
                  pl.BlockSpec((None, None, rows, HEAD_DIM), lambda bi, i, pt: (layer, pt[bi, 2 * i + 1], 0, 0)),
                  pl.BlockSpec((2, 8, feat), lambda bi, i, pt: (0, 0, 0)),
                  pl.BlockSpec((2, feat, NSA_CMP_HIDDEN), lambda bi, i, pt: (0, 0, 0)),
                  pl.BlockSpec((2, 1, NSA_CMP_HIDDEN), lambda bi, i, pt: (0, 0, 0)),
                  pl.BlockSpec((2, NSA_CMP_HIDDEN, HEAD_DIM), lambda bi, i, pt: (0, 0, 0)),
                  pl.BlockSpec((2, 1, HEAD_DIM), lambda bi, i, pt: (0, 0, 0))],
        out_specs=pl.BlockSpec((None, 2, NSA_GROUPS, n_ch, HEAD_DIM), lambda bi, i, pt: (bi, 0, 0, 0, 0)),
        scratch_shapes=[pltpu.VMEM((2 * NSA_GROUPS, n_ch, NSA_CMP_STRIDE * HEAD_DIM), jnp.bfloat16)])
    return pl.pallas_call(
        functools.partial(_compress_pages_kernel, n_steps=pages // 2),
        grid_spec=grid_spec,
        out_shape=jax.ShapeDtypeStruct((b, 2, NSA_GROUPS, n_ch, HEAD_DIM), jnp.float32),
        compiler_params=pltpu.CompilerParams(
            dimension_semantics=("parallel", "arbitrary"), vmem_limit_bytes=VMEM_LIMIT),
    )(page_table, cache2d, cache2d, pe, cmp_w1.reshape(2, feat, NSA_CMP_HIDDEN).astype(jnp.bfloat16),
      cmp_b1.reshape(2, 1, NSA_CMP_HIDDEN), cmp_w2.astype(jnp.bfloat16), cmp_b2.reshape(2, 1, HEAD_DIM))


def _nsa_sel_pages_kernel(pt_ref, q_ref, pg_ref, km_ref, bias_ref, o_ref, m_ref, l_ref, *, last_page, group_shift):
    p = pl.program_id(1)
    rpt = NSA_ROWS_PER_TOKEN
    x = pg_ref[...].astype(jnp.bfloat16)
    s = lax.dot_general(q_ref[...], x, _NT, preferred_element_type=jnp.float32)
    s = s * HEAD_DIM ** -0.5 + bias_ref[_is_last(p, last_page)]
    tokens = km_ref.shape[1]
    tok = lax.broadcasted_iota(jnp.int32, (tokens, s.shape[1]), 0)
    ln = lax.broadcasted_iota(jnp.int32, (tokens, s.shape[1]), 1)
    spread = ((ln >> 3) == tok).astype(jnp.float32)
    keep = jnp.dot(km_ref[...], spread, preferred_element_type=jnp.float32)
    row = lax.broadcasted_iota(jnp.int32, s.shape, 0)
    lane = lax.broadcasted_iota(jnp.int32, s.shape, 1)
    sel_k_row = 2 * NSA_GROUPS + (row >> group_shift)
    s = jnp.where((keep > 0.5) & ((lane & (rpt - 1)) == sel_k_row), s, -jnp.inf)
    m, e, l = _page_softmax(s)
    o_ref[...] = jnp.dot(pltpu.roll(e, NSA_GROUPS, 1).astype(jnp.bfloat16), x, preferred_element_type=jnp.float32)
    m_ref[...] = m
    l_ref[...] = l


def nsa_sel_pages(q_rows, cache2d, layer, page_table, key_mask, bias2):
    b, n_rows, _ = q_rows.shape
    pages = page_table.shape[1]
    rows = cache2d.shape[2]
    page = rows // NSA_ROWS_PER_TOKEN
    rows_per_group = n_rows // NSA_GROUPS
    assert NSA_ROWS_PER_TOKEN == 8 and rows_per_group & (rows_per_group - 1) == 0
    grid_spec = pltpu.PrefetchScalarGridSpec(
        num_scalar_prefetch=1, grid=(b, pages),
        in_specs=[pl.BlockSpec((None, n_rows, HEAD_DIM), lambda bi, p, pt: (bi, 0, 0)),
                  pl.BlockSpec((None, None, rows, HEAD_DIM), lambda bi, p, pt: (layer, pt[bi, p], 0, 0)),
                  pl.BlockSpec((None, n_rows, page), lambda bi, p, pt: (bi, 0, p)),
                  pl.BlockSpec((2, n_rows, rows), lambda bi, p, pt: (0, 0, 0))],
        out_specs=_page_stat_specs(n_rows, HEAD_DIM))
    return pl.pallas_call(
        functools.partial(_nsa_sel_pages_kernel, last_page=pages - 1,
                          group_shift=rows_per_group.bit_length() - 1),
        grid_spec=grid_spec,
        out_shape=_page_stat_shapes(b, pages, n_rows, HEAD_DIM),
        compiler_params=_PAGE_PARAMS,
    )(page_table, q_rows, cache2d, key_mask, bias2)


def _merge_pages(m_pg, l_pg, o_pg, w_pg, s_own, v_own):
    m_sel = jnp.where(w_pg, m_pg, -jnp.inf)
    top = jnp.maximum(jnp.max(m_sel, axis=-1), jnp.max(s_own, axis=-1))[..., None]
    wp = jnp.exp(m_sel - top)
    e_own = jnp.exp(s_own - top)
    den = jnp.sum(wp * l_pg, axis=-1) + jnp.sum(e_own, axis=-1)
    num = jnp.sum(wp[..., None] * o_pg, axis=-2) + jnp.sum(e_own[..., None] * v_own, axis=-2)
    return num / den[..., None]


def _decode_bias(strip, t, page):
    assert page >= REL_MAX_DIST
    n_ab = MOBA_HEADS + DIFF_HEADS
    d_last = jnp.clip(page + jnp.arange(t)[:, None] - jnp.arange(page)[None, :], 0, REL_MAX_DIST)
    last = strip[d_last]
    far = jnp.broadcast_to(strip[REL_MAX_DIST], last.shape)
    out = []
    for src in (far, last):
        a = jnp.transpose(src[:, :, :MOBA_HEADS], (0, 2, 1)).reshape(t * MOBA_HEADS, page)
        a = jnp.repeat(a, MOBA_HEADS, axis=1)
        bb = jnp.transpose(src[:, :, MOBA_HEADS:n_ab], (2, 0, 1))
        bb = jnp.broadcast_to(bb[:, None], (DIFF_HEADS, 2, t, page)).reshape(DIFF_HEADS * 2 * t, page)
        c = src[:, :, n_ab:].reshape(t, page, NSA_GROUPS, NSA_HPG)
        c = jnp.transpose(c, (2, 0, 3, 1)).reshape(NSA_GROUPS * t * NSA_HPG, page)
        c = jnp.repeat(c, NSA_ROWS_PER_TOKEN, axis=1)
        out.append((a, bb, c))
    return tuple(jnp.stack([out[0][i], out[1][i]]) for i in range(3))


def _masked_softmax(s, mask, axis):
    s = jnp.where(mask, s, -jnp.inf)
    m = jnp.max(s, axis=axis, keepdims=True)
    m = jnp.where(jnp.isfinite(m), m, 0.0)
    e = jnp.where(mask, jnp.exp(s - m), 0.0)
    return e / jnp.maximum(jnp.sum(e, axis=axis, keepdims=True), 1e-30)


def _topk_mask(score, n_top):
    nb = score.shape[-1]
    iota = lax.broadcasted_iota(jnp.int32, score.shape, score.ndim - 1)
    sel = jnp.zeros(score.shape, bool)
    for _ in range(n_top):
        m = jnp.max(score, axis=-1, keepdims=True)
        first = jnp.min(jnp.where(score == m, iota, nb), axis=-1, keepdims=True)
        hit = (iota == first) & (m > -jnp.inf)
        sel = sel | hit
        score = jnp.where(iota == first, -jnp.inf, score)
    return sel


def _bias_of_dist(strip, q_pos, k_pos):
    d = jnp.clip(q_pos[:, None] - k_pos[None, :], 0, REL_MAX_DIST)
    return strip[d]


def _own_logits(s, strip_h, t):
    tq = jnp.arange(t)
    dist = tq[:, None] - tq[None, :]
    s = s * HEAD_DIM ** -0.5 + strip_h[jnp.clip(dist, 0, REL_MAX_DIST)]
    return jnp.where((dist >= 0)[..., None], s, -jnp.inf)


def _decode_moba(p3, q_start, layer, cache, page_table, bias2, strip_a):
    b, t, _ = p3.shape
    h, d = MOBA_HEADS, HEAD_DIM
    pages = page_table.shape[1]
    page = cache.shape[2]
    ppb = MOBA_BLOCK // page
    assert q_start == pages * page and q_start % MOBA_BLOCK == 0 and t <= MOBA_BLOCK and MOBA_BLOCK % page == 0
    q = p3[:, :, C_MQ:C_MQ + _BW].reshape(b, t, h, d)
    k_new = p3[:, :, C_MK:C_MK + _BW].reshape(b, t, h, d)
    v_new = p3[:, :, C_MV:C_MV + _BW].reshape(b, t, h, d)
    o_pg, m_pg, l_pg, ksum = moba_pages(q.reshape(b, t * h, d).astype(jnp.bfloat16), cache, layer, page_table, bias2)
    k_mean = jnp.sum(ksum.reshape(b, pages // ppb, ppb, h, d), axis=2) / MOBA_BLOCK
    score = jnp.einsum('bthd,bnhd->bthn', q, k_mean, precision=lax.Precision.HIGHEST)
    sel = _topk_mask(score, min(MOBA_TOPK, q_start // MOBA_BLOCK))
    w_pg = jnp.repeat(sel, ppb, axis=-1)

    def per_query(a):
        return jnp.transpose(a.reshape(b, pages, t, h, a.shape[-1]), (0, 2, 3, 1, 4))

    s_own = _own_logits(jnp.einsum('bthd,bshd->btsh', q, k_new), strip_a, t)
    s_own = jnp.transpose(s_own, (0, 1, 3, 2))
    v_own = jnp.transpose(v_new, (0, 2, 1, 3))[:, None]
    o = _merge_pages(per_query(m_pg)[..., 0], per_query(l_pg)[..., 0], per_query(o_pg), w_pg, s_own, v_own)
    return o.reshape(b, t, _BW)


def _decode_diff(p3, q_start, layer, cache2d, page_table, bias2, strip_b, lam, subln_g, lam_init):
    b, t, _ = p3.shape
    h, d = DIFF_HEADS, HEAD_DIM
    pages = page_table.shape[1]
    assert q_start == pages * cache2d.shape[2]
    q = p3[:, :, C_DQ:C_DQ + _BW].reshape(b, t, h, 2, d)
    k_new = p3[:, :, C_DK:C_DK + _BW].reshape(b, t, h, 2, d)
    v_new = p3[:, :, C_DV:C_DV + _BW].reshape(b, t, h, DIFF_VDIM)
    q_rows = jnp.transpose(q, (0, 2, 3, 1, 4)).reshape(b, 2 * h, t, 1, d)
    q_blk = (q_rows * jnp.eye(2 * h, dtype=q.dtype)[None, :, None, :, None]).reshape(b, 2 * h * t, 2 * h * d)
    o_pg, m_pg, l_pg = diff_pages(q_blk.astype(jnp.bfloat16), cache2d, layer, page_table, bias2)

    def per_query(a):
        return jnp.transpose(a.reshape(b, pages, h, 2, t, a.shape[-1]), (0, 2, 3, 4, 1, 5))

    s_own = _own_logits(jnp.einsum('bthmd,bshmd->bmtsh', q, k_new), strip_b, t)
    s_own = jnp.transpose(s_own, (0, 4, 1, 2, 3))
    v_own = jnp.transpose(v_new, (0, 2, 1, 3))[:, :, None, None]
    w_pg = jnp.ones((b, h, 2, t, pages), bool)
    o = _merge_pages(per_query(m_pg)[..., 0], per_query(l_pg)[..., 0], per_query(o_pg), w_pg, s_own, v_own)
    o = jnp.transpose(o[:, :, 0] - lam * o[:, :, 1], (0, 2, 1, 3))
    o = o * lax.rsqrt(jnp.mean(o * o, axis=-1, keepdims=True) + SUBLN_EPS) * subln_g * (1.0 - lam_init)
    return o.reshape(b, t, _BW)


def _cmp_attention_select(q, q_start, k_cmp, v_cmp, l):
    b, t, g, j, d = q.shape
    n_cmp = k_cmp.shape[2]
    q_pos = q_start + jnp.arange(t)
    cmp_end = jnp.arange(n_cmp) * NSA_CMP_STRIDE + NSA_CMP_BLOCK - 1
    valid = cmp_end[None, :] <= q_pos[:, None]
    s = jnp.einsum('btgjd,bgnd->bgjtn', q, k_cmp, precision=lax.Precision.HIGHEST) * d ** -0.5
    p = _masked_softmax(s, valid, -1)
    o_cmp = jnp.einsum('bgjtn,bgnd->btgjd', p, v_cmp)
    imp = jnp.sum(p, axis=2)
    nbs = -(-l // NSA_SEL_BLOCK)
    ratio = NSA_SEL_BLOCK // NSA_CMP_STRIDE
    front = NSA_CMP_BLOCK // NSA_CMP_STRIDE - 1
    span = ratio + front
    back = max(ratio * (nbs - 1) + span - front - n_cmp, 0)
    imp = jnp.pad(imp, ((0, 0), (0, 0), (0, 0), (front, back)))
    slc = sum(imp[..., o:o + ratio * nbs:ratio] for o in range(span))
    slc = jnp.transpose(slc, (0, 2, 1, 3))
    own = q_pos // NSA_SEL_BLOCK
    blk = jnp.arange(nbs)
    elig = blk[None, :] < own[:, None]
    forced = (blk[None, :] == 0) | (blk[None, :] == own[:, None] - 1)
    score = jnp.where(forced[None, :, None, :], jnp.inf, slc)
    score = jnp.where(elig[None, :, None, :], score, -jnp.inf)
    n_top = min(NSA_SEL_COUNT - 1, (q_start + t - 1) // NSA_SEL_BLOCK)
    sel = _topk_mask(score, n_top) | (blk[None, :] == own[:, None])[None, :, None, :]
    return o_cmp, sel


def _decode_nsa(p3, q_start, layer, cache2d, page_table, bias2, strip_c, win_buf, cmp_w):
    b, t, _ = p3.shape
    g, j, d = NSA_GROUPS, NSA_HPG, HEAD_DIM
    pages = page_table.shape[1]
    page = cache2d.shape[2] // NSA_ROWS_PER_TOKEN
    past = pages * page
    assert q_start == past and past % NSA_SEL_BLOCK == 0 and t <= NSA_SEL_BLOCK and (q_start + t) // NSA_CMP_STRIDE == past // NSA_CMP_STRIDE
    q = p3[:, :, C_NQ:C_NQ + _BW].reshape(b, t, g, j, d)
    kv_cmp = compress_pages(cache2d, layer, page_table, *cmp_w)
    n_cmp = past // NSA_CMP_STRIDE - NSA_CMP_BLOCK // NSA_CMP_STRIDE + 1
    o_cmp, sel = _cmp_attention_select(q, q_start, kv_cmp[:, 0, :, :n_cmp], kv_cmp[:, 1, :, :n_cmp], q_start + t)
    key_mask = jnp.repeat(sel[..., :past // NSA_SEL_BLOCK], NSA_SEL_BLOCK, axis=-1)
    key_mask = jnp.broadcast_to(jnp.transpose(key_mask, (0, 2, 1, 3))[:, :, :, None], (b, g, t, j, past))
    q_rows = jnp.transpose(q, (0, 2, 1, 3, 4)).reshape(b, g * t * j, d)
    o_pg, m_pg, l_pg = nsa_sel_pages(q_rows.astype(jnp.bfloat16), cache2d, layer, page_table,
                                     key_mask.reshape(b, g * t * j, past).astype(jnp.float32), bias2)

    def per_query(a):
        return jnp.transpose(a.reshape(b, pages, g, t, j, a.shape[-1]), (0, 2, 3, 4, 1, 5))

    sk_new = p3[:, :, C_SK:C_SK + _GW].reshape(b, t, g, d)
    sv_new = p3[:, :, C_SV:C_SV + _GW].reshape(b, t, g, d)
    s_own = _own_logits(jnp.einsum('btgjd,bsgd->btsgj', q, sk_new).reshape(b, t, t, g * j), strip_c, t)
    s_own = jnp.transpose(s_own.reshape(b, t, t, g, j), (0, 3, 1, 4, 2))
    v_own = jnp.transpose(sv_new, (0, 2, 1, 3))[:, :, None, None]
    w_pg = jnp.ones((b, g, t, j, pages), bool)
    o_sel = _merge_pages(per_query(m_pg)[..., 0], per_query(l_pg)[..., 0], per_query(o_pg), w_pg, s_own, v_own)
    o_sel = jnp.transpose(o_sel, (0, 2, 1, 3, 4))

    win_rows = p3[:, :, C_WK:C_WK + 2 * _GW].reshape(b, t, 2, g, d)
    w = win_buf.shape[1]
    kv = jnp.concatenate([win_buf, win_rows], axis=1)
    o_win = window_attention(q, q_start + jnp.arange(t), kv[:, :, 0], kv[:, :, 1],
                             q_start - w + jnp.arange(w + t), strip_c)
    g_nsa = jax.nn.sigmoid(p3[:, :, C_NG:C_NG + N_NG].reshape(b, t, g, j, 3))
    o_c = g_nsa[..., 0:1] * o_cmp + g_nsa[..., 1:2] * o_sel + g_nsa[..., 2:3] * o_win
    return o_c.reshape(b, t, _BW), kv[:, t:]


def window_attention(q, q_pos, k, v, k_pos, strip):
    b, t, g, j, d = q.shape
    l = k.shape[1]
    dist = q_pos[:, None] - k_pos[None, :]
    mask = (k_pos[None, :] >= 0) & (dist >= 0) & (dist < NSA_WINDOW)
    bias = jnp.transpose(_bias_of_dist(strip, q_pos, k_pos).reshape(t, l, g, j), (2, 3, 0, 1))
    s = jnp.einsum('btgjd,bsgd->bgjts', q, k) * d ** -0.5 + bias[None]
    p = _masked_softmax(s, mask, -1)
    return jnp.einsum('bgjts,bsgd->btgjd', p, v)


def _layer(x, q_start, past, win_buf, conv_buf, bias, lam_init, wts):
    (g_attn, w_in_p, lam_p, subln_g, cmp_pos, cmp_w1, cmp_b1, cmp_w2, cmp_b2,
     w_branch, w_out, g_ffn, w_gate, w_up, w_down, conv_w, conv_b) = wts
    b, t, _ = x.shape
    m = b * t
    hd = HEAD_DIM
    x2 = x.reshape(m, D_MODEL)
    h = rmsnorm_rows(x2, g_attn, RMS_EPS, jnp.bfloat16)
    proj = matmul(h, w_in_p)
    p3 = proj.reshape(b, t, PROJ_PACKED)

    def cols(start, width):
        return p3[:, :, start:start + width]

    moba_rows = cols(C_MK, 2 * _BW).reshape(b, t, 2, MOBA_HEADS, hd)
    diff_rows = cols(C_DK, 2 * _BW).reshape(b, t, 2, DIFF_HEADS, DIFF_VDIM)
    nsa_rows = cols(C_CK, 4 * _GW).reshape(b, t, 4, NSA_GROUPS, hd)
    win_rows = cols(C_WK, 2 * _GW).reshape(b, t, 2, NSA_GROUPS, hd)
    strip, tiles = bias
    n_ab = MOBA_HEADS + DIFF_HEADS
    lam_f = lam_p.astype(jnp.float32)
    lam = jnp.exp(jnp.sum(lam_f[0] * lam_f[1])) - jnp.exp(jnp.sum(lam_f[2] * lam_f[3])) + lam_init
    if past is None:
        o_a = moba_prompt(p3, tiles[:MOBA_HEADS])
        o_b = diff_prompt(p3, tiles[MOBA_HEADS:n_ab], lam, subln_g, lam_init)
        kv_cmp = compress_prompt(p3, cmp_pos, cmp_w1, cmp_b1, cmp_w2, cmp_b2)
        o_cmp, sel_mask = cmp_select_prompt(p3, kv_cmp)
        o_c = nsa_prompt(p3, sel_mask, o_cmp,
                         tiles[n_ab:].reshape(NSA_GROUPS, NSA_HPG, N_BIAS_TILES, ATT_TILE, ATT_TILE))
        new_win = win_rows[:, t - min(NSA_WINDOW, t):]
    else:
        layer, (cache_a, cache_b, cache_c), page_table, (bias_a, bias_b, bias_c) = past
        o_a = _decode_moba(p3, q_start, layer, cache_a, page_table, bias_a, strip[:, :MOBA_HEADS])
        o_b = _decode_diff(p3, q_start, layer, cache_b, page_table, bias_b, strip[:, MOBA_HEADS:n_ab],
                           lam, subln_g, lam_init)
        o_c, new_win = _decode_nsa(p3, q_start, layer, cache_c, page_table, bias_c, strip[:, n_ab:], win_buf,
                                   (cmp_pos, cmp_w1, cmp_b1, cmp_w2, cmp_b2))
    branches = [o.reshape(m, _BW).astype(jnp.bfloat16) for o in (o_a, o_b, o_c)]
    mix = merge_branches(branches, w_branch, proj)
    x1 = matmul(mix, w_out, x2)

    hf = rmsnorm_rows(x1, g_ffn, RMS_EPS, jnp.bfloat16)
    if conv_buf is None:
        tm = 512
        hprev = hf.reshape(b, t // tm, tm, D_MODEL)[:, :, tm - 8:, :]
        hprev = jnp.concatenate([jnp.zeros_like(hprev[:, :1]), hprev[:, :-1]], axis=1)
        gprev = matmul(hprev.reshape(m // tm * 8, D_MODEL), w_gate).reshape(m // tm, 8, D_FF_PAD)
        act = ffn_in(hf, gprev, w_gate, w_up, conv_w, conv_b, tm=tm)
        g_last = matmul(hf.reshape(b, t, D_MODEL)[:, t - 8:].reshape(b * 8, D_MODEL), w_gate)
        new_conv = g_last.reshape(b, 8, D_FF_PAD)[:, 8 - (CONV_WIDTH - 1):, :D_FF]
    else:
        g = matmul(hf, w_gate).reshape(b, t, D_FF_PAD)
        u = matmul(hf, w_up)
        prev = jnp.pad(conv_buf, ((0, 0), (0, 0), (0, D_FF_PAD - D_FF)))
        gp = jnp.concatenate([prev, g], axis=1)
        shifted = [gp[:, i:i + t].reshape(m, D_FF_PAD) for i in range(CONV_WIDTH)]
        act = conv_act(*shifted, u, conv_w, conv_b)
        new_conv = gp[:, t:, :D_FF]
    x2o = matmul(act, w_down, x1, tm=1024, tn=1024, tk=2816)
    return x2o.reshape(b, t, D_MODEL), moba_rows, diff_rows, nsa_rows, new_win, new_conv


def _page_views(cache_moba_kv, cache_diff_kv, cache_nsa_kv):
    dd, n_pool, page = cache_diff_kv.shape[:3]
    diff2d = cache_diff_kv.reshape(dd, n_pool, page, -1).astype(jnp.bfloat16)
    nd = cache_nsa_kv.shape[0]
    nsa2d = cache_nsa_kv.reshape(nd, cache_nsa_kv.shape[1], page * NSA_ROWS_PER_TOKEN, HEAD_DIM)
    return cache_moba_kv, diff2d, nsa2d


def _pack_w_in(w_in):
    head = w_in[:, :, :C_BG_SRC]
    head = jnp.pad(head, ((0, 0), (0, 0), (0, NG_PAD - N_NG)))
    return jnp.concatenate([head, w_in[:, :, C_BG_SRC:]], axis=2).astype(jnp.bfloat16)


def kernel(x_prompt, x_sample, cache_moba_kv, cache_diff_kv, cache_nsa_kv, state_nsa_win, state_ffn_conv, page_table, rel_bias, norm_attn, w_in, diff_lambda, diff_subln, nsa_cmp_pos, nsa_cmp_w1, nsa_cmp_b1, nsa_cmp_w2, nsa_cmp_b2, w_branch, w_out, norm_ffn, w_gate, w_up, w_down, conv_w, conv_b, norm_final):
    depth = w_in.shape[0]
    past_len = page_table.shape[1] * cache_moba_kv.shape[2]
    bf = jnp.bfloat16
    fpad = D_FF_PAD - D_FF
    w_in_p = _pack_w_in(w_in)
    w_branch_b = w_branch.astype(bf)
    w_out_b = w_out.astype(bf)
    w_gate_b = jnp.pad(w_gate, ((0, 0), (0, 0), (0, fpad))).astype(bf)
    w_up_b = jnp.pad(w_up, ((0, 0), (0, 0), (0, fpad))).astype(bf)
    w_down_b = jnp.pad(w_down, ((0, 0), (0, fpad), (0, 0))).astype(bf)
    conv_w_p = jnp.pad(conv_w, ((0, 0), (0, 0), (0, fpad)))
    conv_b_p = jnp.pad(conv_b, ((0, 0), (0, fpad))).reshape(depth, 1, D_FF_PAD)
    strip = rel_bias[jnp.asarray(_BUCKET)]
    bias = (strip, _bias_tiles(strip))
    caches = _page_views(cache_moba_kv, cache_diff_kv, cache_nsa_kv)
    dec_bias = _decode_bias(strip, x_sample.shape[1], cache_moba_kv.shape[2])

    y_p, y_s = x_prompt, x_sample
    outs_p = [[] for _ in range(5)]
    outs_s = [[] for _ in range(5)]
    for layer in range(depth):
        lam_init = 0.8 - 0.6 * math.exp(-0.3 * layer)
        wts = (norm_attn[layer], w_in_p[layer], diff_lambda[layer], diff_subln[layer], nsa_cmp_pos[layer],
               nsa_cmp_w1[layer], nsa_cmp_b1[layer], nsa_cmp_w2[layer], nsa_cmp_b2[layer], w_branch_b[layer],
               w_out_b[layer], norm_ffn[layer], w_gate_b[layer], w_up_b[layer], w_down_b[layer],
               conv_w_p[layer], conv_b_p[layer])
        y_p, *new_p = _layer(y_p, 0, None, None, None, bias, lam_init, wts)

        past = (layer, caches, page_table, dec_bias)
        y_s, *new_s = _layer(y_s, past_len, past, state_nsa_win[layer], state_ffn_conv[layer], bias, lam_init, wts)
        for lst, a in zip(outs_p, new_p):
            lst.append(a)
        for lst, a in zip(outs_s, new_s):
            lst.append(a)
    bp, tp, _ = y_p.shape
    bs, ts, _ = y_s.shape
    y_p = rmsnorm_rows(y_p.reshape(bp * tp, D_MODEL), norm_final, RMS_EPS, jnp.float32).reshape(bp, tp, D_MODEL)
    y_s = rmsnorm_rows(y_s.reshape(bs * ts, D_MODEL), norm_final, RMS_EPS, jnp.float32).reshape(bs, ts, D_MODEL)
    sp = [jnp.stack(l) for l in outs_p]
    ss = [jnp.stack(l) for l in outs_s]
    return (y_p, y_s, sp[0], ss[0], sp[1], ss[1], sp[2], ss[2], sp[3], ss[3], sp[4], ss[4])
```

```python
import functools
import math

import numpy as np
import jax
import jax.numpy as jnp
from jax import lax
from jax.experimental import pallas as pl
from jax.experimental.pallas import tpu as pltpu

D_MODEL = 4096
HEAD_DIM = 128
MOBA_HEADS = 8
MOBA_BLOCK = 256
MOBA_TOPK = 3
DIFF_HEADS = 4
DIFF_VDIM = 2 * HEAD_DIM
NSA_HEADS = 8
NSA_GROUPS = 2
NSA_HPG = NSA_HEADS // NSA_GROUPS
NSA_CMP_BLOCK = 32
NSA_CMP_STRIDE = 16
NSA_CMP_HIDDEN = 2 * HEAD_DIM
NSA_SEL_BLOCK = 64
NSA_SEL_COUNT = 16
NSA_WINDOW = 512
REL_BUCKETS = 32
REL_MAX_DIST = 128
N_BRANCH = 3
BRANCH_WIDTH = MOBA_HEADS * HEAD_DIM
D_FF = 11008
CONV_WIDTH = 3
RMS_EPS = 1e-6
SUBLN_EPS = 1e-5

_BW = BRANCH_WIDTH
_GW = NSA_GROUPS * HEAD_DIM
C_MQ, C_MK, C_MV = 0, _BW, 2 * _BW
C_DQ, C_DK, C_DV = 3 * _BW, 4 * _BW, 5 * _BW
C_NQ = 6 * _BW
C_CK = 7 * _BW
C_CV, C_SK, C_SV, C_WK, C_WV = (C_CK + _GW, C_CK + 2 * _GW, C_CK + 3 * _GW,
                                C_CK + 4 * _GW, C_CK + 5 * _GW)
C_NG = C_CK + 6 * _GW
N_NG = 3 * NSA_HEADS
C_BG_SRC = C_NG + N_NG
NG_PAD = 512
C_BG = C_NG + NG_PAD
PROJ_PACKED = C_BG + N_BRANCH * D_MODEL
D_FF_PAD = 11264

VMEM_LIMIT = 56 * 1024 * 1024


def _bucket_of_dist():
    n = np.arange(REL_MAX_DIST + 1)
    exact = REL_BUCKETS // 2
    nf = np.maximum(n, 1).astype(np.float64)
    large = exact + (np.log(nf / exact) / math.log(REL_MAX_DIST / exact) * (REL_BUCKETS - exact)).astype(np.int64)
    return np.where(n < exact, n, np.minimum(large, REL_BUCKETS - 1)).astype(np.int32)


_BUCKET = _bucket_of_dist()


def _rmsnorm_kernel(x_ref, g_ref, o_ref, *, eps):
    x = x_ref[...]
    ms = jnp.mean(x * x, axis=-1, keepdims=True)
    o_ref[...] = (x * lax.rsqrt(ms + eps) * g_ref[...]).astype(o_ref.dtype)


def rmsnorm_rows(x, g, eps, out_dtype):
    m, d = x.shape
    tm = min(m, 256)
    return pl.pallas_call(
        functools.partial(_rmsnorm_kernel, eps=eps),
        grid=(m // tm,),
        in_specs=[pl.BlockSpec((tm, d), lambda i: (i, 0)),
                  pl.BlockSpec((1, d), lambda i: (0, 0))],
        out_specs=pl.BlockSpec((tm, d), lambda i: (i, 0)),
        out_shape=jax.ShapeDtypeStruct((m, d), out_dtype),
        compiler_params=pltpu.CompilerParams(dimension_semantics=("parallel",)),
    )(x, g.reshape(1, d).astype(jnp.float32))


def _matmul_kernel(a_ref, w_ref, *rest, nk, has_res):
    if has_res:
        r_ref, o_ref = rest
    else:
        (o_ref,) = rest
    part = jnp.dot(a_ref[...], w_ref[...], preferred_element_type=jnp.float32)
    if nk == 1:
        o_ref[...] = part + r_ref[...] if has_res else part
        return
    k = pl.program_id(2)

    @pl.when(k == 0)
    def _():
        o_ref[...] = part + r_ref[...] if has_res else part

    @pl.when(k > 0)
    def _():
        o_ref[...] += part


def matmul(a, w, res=None, *, tm=512, tn=1024, tk=None):
    m, kd = a.shape
    n = w.shape[1]
    tm = min(tm, m)
    tn = min(tn, n)
    tk = kd if tk is None else tk
    nk = kd // tk
    in_specs = [pl.BlockSpec((tm, tk), lambda j, i, k: (i, k)),
                pl.BlockSpec((tk, tn), lambda j, i, k: (k, j))]
    args = [a, w]
    if res is not None:
        in_specs.append(pl.BlockSpec((tm, tn), lambda j, i, k: (i, j)))
        args.append(res)
    return pl.pallas_call(
        functools.partial(_matmul_kernel, nk=nk, has_res=res is not None),
        grid=(n // tn, m // tm, nk),
        in_specs=in_specs,
        out_specs=pl.BlockSpec((tm, tn), lambda j, i, k: (i, j)),
        out_shape=jax.ShapeDtypeStruct((m, n), jnp.float32),
        compiler_params=pltpu.CompilerParams(
            dimension_semantics=("parallel", "parallel", "arbitrary"),
            vmem_limit_bytes=VMEM_LIMIT),
    )(*args)


def _merge_kernel(b0_ref, b1_ref, b2_ref, wb_ref, g0_ref, g1_ref, g2_ref, o_ref):
    acc = None
    for m, (b_ref, g_ref) in enumerate(((b0_ref, g0_ref), (b1_ref, g1_ref), (b2_ref, g2_ref))):
        u = jnp.dot(b_ref[...], wb_ref[m], preferred_element_type=jnp.float32)
        t = jax.nn.sigmoid(g_ref[...]) * u
        acc = t if acc is None else acc + t
    o_ref[...] = acc.astype(o_ref.dtype)


def merge_branches(branches, w_branch, proj, *, tm=512, tn=1024):
    m, bw = branches[0].shape
    d = w_branch.shape[2]
    tm = min(tm, m)
    gate_specs = [
        pl.BlockSpec((tm, tn), functools.partial(
            lambda j, i, base: (i, base + j), base=(C_BG + mm * d) // tn))
        for mm in range(N_BRANCH)]
    branch_spec = pl.BlockSpec((tm, bw), lambda j, i: (i, 0))
    return pl.pallas_call(
        _merge_kernel,
        grid=(d // tn, m // tm),
        in_specs=[branch_spec] * N_BRANCH + [pl.BlockSpec((N_BRANCH, bw, tn), lambda j, i: (0, 0, j))] + gate_specs,
        out_specs=pl.BlockSpec((tm, tn), lambda j, i: (i, j)),
        out_shape=jax.ShapeDtypeStruct((m, d), jnp.bfloat16),
        compiler_params=pltpu.CompilerParams(
            dimension_semantics=("parallel", "parallel"), vmem_limit_bytes=VMEM_LIMIT),
    )(*branches, w_branch, proj, proj, proj)


def _ffn_in_kernel(h_ref, gprev_ref, wg_ref, wu_ref, cw_ref, cb_ref, o_ref):
    h = h_ref[...]
    g = jnp.dot(h, wg_ref[...], preferred_element_type=jnp.float32)
    u = jnp.dot(h, wu_ref[...], preferred_element_type=jnp.float32)
    tm = g.shape[0]
    gcat = jnp.concatenate([gprev_ref[0], g], axis=0)
    cw = cw_ref[...]
    conv = (gcat[6:6 + tm] * cw[0:1] + gcat[7:7 + tm] * cw[1:2] + g * cw[2:3] + cb_ref[...])
    o_ref[...] = (jax.nn.silu(conv) * u).astype(o_ref.dtype)


def ffn_in(h, gprev, wg, wu, conv_w, conv_b, *, tm=512, tf=512):
    m, d = h.shape
    f = wg.shape[1]
    return pl.pallas_call(
        _ffn_in_kernel,
        grid=(f // tf, m // tm),
        in_specs=[pl.BlockSpec((tm, d), lambda j, i: (i, 0)),
                  pl.BlockSpec((1, 8, tf), lambda j, i: (i, 0, j)),
                  pl.BlockSpec((d, tf), lambda j, i: (0, j)),
                  pl.BlockSpec((d, tf), lambda j, i: (0, j)),
                  pl.BlockSpec((CONV_WIDTH, tf), lambda j, i: (0, j)),
                  pl.BlockSpec((1, tf), lambda j, i: (0, j))],
        out_specs=pl.BlockSpec((tm, tf), lambda j, i: (i, j)),
        out_shape=jax.ShapeDtypeStruct((m, f), jnp.bfloat16),
        compiler_params=pltpu.CompilerParams(
            dimension_semantics=("parallel", "parallel"), vmem_limit_bytes=VMEM_LIMIT),
    )(h, gprev, wg, wu, conv_w, conv_b)


def _conv_act_kernel(g0_ref, g1_ref, g2_ref, u_ref, cw_ref, cb_ref, o_ref):
    cw = cw_ref[...]
    conv = g0_ref[...] * cw[0:1] + g1_ref[...] * cw[1:2] + g2_ref[...] * cw[2:3] + cb_ref[...]
    o_ref[...] = (jax.nn.silu(conv) * u_ref[...]).astype(o_ref.dtype)


def conv_act(g0, g1, g2, u, conv_w, conv_b, *, tf=1024):
    m, f = u.shape
    row = pl.BlockSpec((m, tf), lambda j: (0, j))
    return pl.pallas_call(
        _conv_act_kernel,
        grid=(f // tf,),
        in_specs=[row, row, row, row,
                  pl.BlockSpec((CONV_WIDTH, tf), lambda j: (0, j)),
                  pl.BlockSpec((1, tf), lambda j: (0, j))],
        out_specs=row,
        out_shape=jax.ShapeDtypeStruct((m, f), jnp.bfloat16),
        compiler_params=pltpu.CompilerParams(dimension_semantics=("parallel",)),
    )(g0, g1, g2, u, conv_w, conv_b)


ATT_TILE = 256
N_BIAS_TILES = 3
_HI = lax.Precision.HIGHEST
_NT = (((1,), (1,)), ((), ()))


def _bias_tiles(strip):
    t = ATT_TILE
    n = 2 * t
    k = np.arange(n)
    k = np.where(k < t, k, k - n)
    d = np.stack([np.clip(o * t - k, 0, REL_MAX_DIST) for o in range(N_BIAS_TILES)])
    v = jnp.transpose(strip[jnp.asarray(d)], (2, 0, 1))
    rows = jnp.tile(v, (1, 1, t))[:, :, :t * (n - 1)].reshape(v.shape[0], N_BIAS_TILES, t, n - 1)
    return rows[:, :, :, :t]


def _tile_iotas():
    r = lax.broadcasted_iota(jnp.int32, (ATT_TILE, ATT_TILE), 0)
    c = lax.broadcasted_iota(jnp.int32, (ATT_TILE, ATT_TILE), 1)
    return r, c


def _causal_tile():
    r, c = _tile_iotas()
    return (r >= c).astype(jnp.float32)


def _attend(nblk, cond, score_fn, v_fn, s_ref, m_ref, l_ref, acc_ref):
    t = ATT_TILE
    m_ref[...] = jnp.full(m_ref.shape, -jnp.inf, jnp.float32)
    for n in range(nblk):
        @pl.when(cond(n))
        def _(n=n):
            s = score_fn(n)
            s_ref[:, n * t:(n + 1) * t] = s
            m_ref[...] = jnp.maximum(m_ref[...], jnp.max(s, axis=-1, keepdims=True))
    m = m_ref[...]
    m = jnp.where(m == -jnp.inf, 0.0, m)
    l_ref[...] = jnp.zeros(l_ref.shape, jnp.float32)
    acc_ref[...] = jnp.zeros(acc_ref.shape, jnp.float32)
    for n in range(nblk):
        @pl.when(cond(n))
        def _(n=n):
            e = jnp.exp(s_ref[:, n * t:(n + 1) * t] - m)
            l_ref[...] += jnp.sum(e, axis=-1, keepdims=True)
            acc_ref[...] += jnp.dot(e.astype(jnp.bfloat16), v_fn(n), preferred_element_type=jnp.float32)
    return acc_ref[...] / jnp.maximum(l_ref[...], 1e-30)


def _select_top(score, n_top):
    lane = lax.broadcasted_iota(jnp.int32, score.shape, 1)
    sel = jnp.zeros(score.shape, jnp.float32)
    for _ in range(n_top):
        m = jnp.max(score, axis=-1, keepdims=True)
        first = jnp.min(jnp.where(score == m, lane, score.shape[1]), axis=-1, keepdims=True)
        hit = lane == first
        sel = jnp.where(hit & (m > -jnp.inf), 1.0, sel)
        score = jnp.where(hit, -jnp.inf, score)
    return sel


def _moba_prompt_kernel(q_ref, k_ref, v_ref, bias_ref, o_ref, kmean_ref, s_ref, m_ref, l_ref, acc_ref, *, nblk):
    t = ATT_TILE
    qi = pl.program_id(2)

    @pl.when(qi == 0)
    def _():
        kmean_ref[...] = jnp.zeros(kmean_ref.shape, jnp.float32)
        for n in range(nblk):
            kmean_ref[n:n + 1, :] = jnp.mean(k_ref[n * t:(n + 1) * t, :], axis=0, keepdims=True)

    q = q_ref[...]
    lane = lax.broadcasted_iota(jnp.int32, (t, HEAD_DIM), 1)
    score = lax.dot_general(q, kmean_ref[...], _NT, precision=_HI, preferred_element_type=jnp.float32)
    score = jnp.where(lane < qi, score, -jnp.inf)
    sel = _select_top(score, min(MOBA_TOPK, nblk - 1))
    q_bf = q.astype(jnp.bfloat16)
    causal = _causal_tile()
    scale = HEAD_DIM ** -0.5

    def score_fn(n):
        s = lax.dot_general(q_bf, k_ref[n * t:(n + 1) * t, :].astype(jnp.bfloat16), _NT,
                            preferred_element_type=jnp.float32)
        s = s * scale + bias_ref[jnp.minimum(qi - n, N_BIAS_TILES - 1)]
        keep = jnp.where(n == qi, causal, jnp.broadcast_to(sel[:, n:n + 1], (t, t)))
        return jnp.where(keep > 0.5, s, -jnp.inf)

    o = _attend(nblk, lambda n: n <= qi, score_fn,
                lambda n: v_ref[n * t:(n + 1) * t, :].astype(jnp.bfloat16), s_ref, m_ref, l_ref, acc_ref)
    o_ref[...] = o.astype(o_ref.dtype)


def moba_prompt(p3, bias_tiles):
    b, t, _ = p3.shape
    nblk = t // ATT_TILE
    hd = HEAD_DIM
    return pl.pallas_call(
        functools.partial(_moba_prompt_kernel, nblk=nblk),
        grid=(b, MOBA_HEADS, nblk),
        in_specs=[pl.BlockSpec((None, ATT_TILE, hd), lambda bi, h, qi: (bi, qi, C_MQ // hd + h)),
                  pl.BlockSpec((None, t, hd), lambda bi, h, qi: (bi, 0, C_MK // hd + h)),
                  pl.BlockSpec((None, t, hd), lambda bi, h, qi: (bi, 0, C_MV // hd + h)),
                  pl.BlockSpec((None, N_BIAS_TILES, ATT_TILE, ATT_TILE), lambda bi, h, qi: (h, 0, 0, 0))],
        out_specs=pl.BlockSpec((None, ATT_TILE, hd), lambda bi, h, qi: (bi, qi, h)),
        out_shape=jax.ShapeDtypeStruct((b, t, BRANCH_WIDTH), jnp.bfloat16),
        scratch_shapes=[pltpu.VMEM((hd, hd), jnp.float32),
                        pltpu.VMEM((ATT_TILE, t), jnp.float32),
                        pltpu.VMEM((ATT_TILE, 1), jnp.float32),
                        pltpu.VMEM((ATT_TILE, 1), jnp.float32),
                        pltpu.VMEM((ATT_TILE, hd), jnp.float32)],
        compiler_params=pltpu.CompilerParams(
            dimension_semantics=("parallel", "parallel", "arbitrary"), vmem_limit_bytes=VMEM_LIMIT),
    )(p3, p3, p3, bias_tiles)


def _diff_prompt_kernel(lam_ref, q_ref, k_ref, v_ref, bias_ref, g_ref, o_ref, s_ref, m_ref, l_ref, acc_ref,
                        *, nblk, out_scale):
    t = ATT_TILE
    qi = pl.program_id(2)
    causal = _causal_tile()
    scale = HEAD_DIM ** -0.5
    outs = []
    for mp in range(2):
        q_bf = q_ref[:, mp * HEAD_DIM:(mp + 1) * HEAD_DIM].astype(jnp.bfloat16)

        def score_fn(n, mp=mp, q_bf=q_bf):
            kb = k_ref[n * t:(n + 1) * t, mp * HEAD_DIM:(mp + 1) * HEAD_DIM].astype(jnp.bfloat16)
            s = lax.dot_general(q_bf, kb, _NT, preferred_element_type=jnp.float32)
            s = s * scale + bias_ref[jnp.minimum(qi - n, N_BIAS_TILES - 1)]
            return jnp.where(jnp.where(n < qi, 1.0, causal) > 0.5, s, -jnp.inf)

        outs.append(_attend(nblk, lambda n: n <= qi, score_fn,
                            lambda n: v_ref[n * t:(n + 1) * t, :].astype(jnp.bfloat16),
                            s_ref, m_ref, l_ref, acc_ref))
    o = outs[0] - lam_ref[0] * outs[1]
    o = o * lax.rsqrt(jnp.mean(o * o, axis=-1, keepdims=True) + SUBLN_EPS) * g_ref[...] * out_scale
    o_ref[...] = o.astype(o_ref.dtype)


def diff_prompt(p3, bias_tiles, lam, subln_g, lam_init):
    b, t, _ = p3.shape
    nblk = t // ATT_TILE
    w = DIFF_VDIM
    return pl.pallas_call(
        functools.partial(_diff_prompt_kernel, nblk=nblk, out_scale=1.0 - lam_init),
        grid=(b, DIFF_HEADS, nblk),
        in_specs=[pl.BlockSpec(memory_space=pltpu.SMEM),
                  pl.BlockSpec((None, ATT_TILE, w), lambda bi, h, qi: (bi, qi, C_DQ // w + h)),
                  pl.BlockSpec((None, t, w), lambda bi, h, qi: (bi, 0, C_DK // w + h)),
                  pl.BlockSpec((None, t, w), lambda bi, h, qi: (bi, 0, C_DV // w + h)),
                  pl.BlockSpec((None, N_BIAS_TILES, ATT_TILE, ATT_TILE), lambda bi, h, qi: (h, 0, 0, 0)),
                  pl.BlockSpec((1, w), lambda bi, h, qi: (0, 0))],
        out_specs=pl.BlockSpec((None, ATT_TILE, w), lambda bi, h, qi: (bi, qi, h)),
        out_shape=jax.ShapeDtypeStruct((b, t, BRANCH_WIDTH), jnp.bfloat16),
        scratch_shapes=[pltpu.VMEM((ATT_TILE, t), jnp.float32),
                        pltpu.VMEM((ATT_TILE, 1), jnp.float32),
                        pltpu.VMEM((ATT_TILE, 1), jnp.float32),
                        pltpu.VMEM((ATT_TILE, w), jnp.float32)],
        compiler_params=pltpu.CompilerParams(
            dimension_semantics=("parallel", "parallel", "parallel"), vmem_limit_bytes=VMEM_LIMIT),
    )(lam.reshape(1).astype(jnp.float32), p3, p3, p3, bias_tiles, subln_g.reshape(1, w).astype(jnp.float32))


def _gelu_tanh(x):
    return 0.5 * x * (1.0 + jnp.tanh(math.sqrt(2.0 / math.pi) * (x + 0.044715 * (x * x * x))))


def _compress_prompt_kernel(x_ref, pe_ref, w1_ref, b1_ref, w2_ref, b2_ref, o_ref, *, n_ch):
    st = NSA_CMP_STRIDE
    pre_a = jnp.zeros((n_ch, NSA_CMP_HIDDEN), jnp.float32)
    pre_b = jnp.zeros((n_ch, NSA_CMP_HIDDEN), jnp.float32)
    for l in range(st):
        x_l = x_ref[pl.ds(l, n_ch, stride=st), :]
        pre_a += jnp.dot((x_l + pe_ref[l:l + 1, :]).astype(jnp.bfloat16), w1_ref[l],
                         preferred_element_type=jnp.float32)
        pre_b += jnp.dot((x_l + pe_ref[st + l:st + l + 1, :]).astype(jnp.bfloat16), w1_ref[st + l],
                         preferred_element_type=jnp.float32)
    pre = pre_a + pltpu.roll(pre_b, n_ch - 1, 0) + b1_ref[...]
    o_ref[...] = jnp.dot(_gelu_tanh(pre).astype(jnp.bfloat16), w2_ref[...],
                         preferred_element_type=jnp.float32) + b2_ref[...]


def compress_prompt(p3, cmp_pos, cmp_w1, cmp_b1, cmp_w2, cmp_b2):
    b, t, _ = p3.shape
    n_ch = t // NSA_CMP_STRIDE
    hd = HEAD_DIM
    g_n = NSA_GROUPS
    return pl.pallas_call(
        functools.partial(_compress_prompt_kernel, n_ch=n_ch),
        grid=(b, 2, g_n),
        in_specs=[pl.BlockSpec((None, t, hd), lambda bi, kv, g: (bi, 0, C_CK // hd + kv * g_n + g)),
                  pl.BlockSpec((None, NSA_CMP_BLOCK, hd), lambda bi, kv, g: (kv, 0, 0)),
                  pl.BlockSpec((None, NSA_CMP_BLOCK, hd, NSA_CMP_HIDDEN), lambda bi, kv, g: (kv, 0, 0, 0)),
                  pl.BlockSpec((None, 1, NSA_CMP_HIDDEN), lambda bi, kv, g: (kv, 0, 0)),
                  pl.BlockSpec((None, NSA_CMP_HIDDEN, hd), lambda bi, kv, g: (kv, 0, 0)),
                  pl.BlockSpec((None, 1, hd), lambda bi, kv, g: (kv, 0, 0))],
        out_specs=pl.BlockSpec((None, None, None, n_ch, hd), lambda bi, kv, g: (bi, kv, g, 0, 0)),
        out_shape=jax.ShapeDtypeStruct((b, 2, g_n, n_ch, hd), jnp.float32),
        compiler_params=pltpu.CompilerParams(
            dimension_semantics=("parallel", "parallel", "parallel"), vmem_limit_bytes=VMEM_LIMIT),
    )(p3, cmp_pos, cmp_w1.astype(jnp.bfloat16), cmp_b1.reshape(2, 1, NSA_CMP_HIDDEN),
      cmp_w2.astype(jnp.bfloat16), cmp_b2.reshape(2, 1, hd))


def _cmp_select_prompt_kernel(q_ref, kc_ref, vc_ref, o_ref, sel_ref, *, n_sel_blocks):
    t = ATT_TILE
    qi = pl.program_id(2)
    ncmp = kc_ref.shape[0]
    row = lax.broadcasted_iota(jnp.int32, (t, ncmp), 0)
    lane = lax.broadcasted_iota(jnp.int32, (t, ncmp), 1)
    q_pos = qi * t + row
    valid = (lane * NSA_CMP_STRIDE + NSA_CMP_BLOCK - 1 <= q_pos) & (lane < ncmp - 1)
    kc = kc_ref[...]
    vc = vc_ref[...].astype(jnp.bfloat16)
    scale = HEAD_DIM ** -0.5
    imp = jnp.zeros((t, ncmp), jnp.float32)
    for j in range(NSA_HPG):
        q = q_ref[:, j * HEAD_DIM:(j + 1) * HEAD_DIM]
        s = lax.dot_general(q, kc, _NT, precision=_HI, preferred_element_type=jnp.float32) * scale
        s = jnp.where(valid, s, -jnp.inf)
        m = jnp.max(s, axis=-1, keepdims=True)
        m = jnp.where(m == -jnp.inf, 0.0, m)
        e = jnp.exp(s - m)
        p = e / jnp.maximum(jnp.sum(e, axis=-1, keepdims=True), 1e-30)
        imp += p
        o_ref[:, j * HEAD_DIM:(j + 1) * HEAD_DIM] = jnp.dot(
            p.astype(jnp.bfloat16), vc, preferred_element_type=jnp.float32).astype(o_ref.dtype)
    ratio = NSA_SEL_BLOCK // NSA_CMP_STRIDE
    front = NSA_CMP_BLOCK // NSA_CMP_STRIDE - 1
    nn = lax.broadcasted_iota(jnp.int32, (ncmp, 128), 0)
    bb = lax.broadcasted_iota(jnp.int32, (ncmp, 128), 1)
    spread = ((nn >= ratio * bb - front) & (nn < ratio * bb + ratio) & (bb < n_sel_blocks)).astype(jnp.float32)
    slc = jnp.dot(imp, spread, precision=_HI, preferred_element_type=jnp.float32)
    row_b = lax.broadcasted_iota(jnp.int32, (t, 128), 0)
    blk = lax.broadcasted_iota(jnp.int32, (t, 128), 1)
    own = (qi * t + row_b) // NSA_SEL_BLOCK
    score = jnp.where((blk == 0) | (blk == own - 1), jnp.inf, slc)
    score = jnp.where(blk < own, score, -jnp.inf)
    sel = _select_top(score, min(NSA_SEL_COUNT - 1, n_sel_blocks - 1))
    sel_ref[...] = jnp.where(blk == own, 1.0, sel)


def cmp_select_prompt(p3, kv_cmp):
    b, t, _ = p3.shape
    nq = t // ATT_TILE
    gw = NSA_HPG * HEAD_DIM
    n_ch = kv_cmp.shape[3]
    return pl.pallas_call(
        functools.partial(_cmp_select_prompt_kernel, n_sel_blocks=t // NSA_SEL_BLOCK),
        grid=(b, NSA_GROUPS, nq),
        in_specs=[pl.BlockSpec((None, ATT_TILE, gw), lambda bi, g, qi: (bi, qi, C_NQ // gw + g)),
                  pl.BlockSpec((None, None, None, n_ch, HEAD_DIM), lambda bi, g, qi: (bi, 0, g, 0, 0)),
                  pl.BlockSpec((None, None, None, n_ch, HEAD_DIM), lambda bi, g, qi: (bi, 1, g, 0, 0))],
        out_specs=[pl.BlockSpec((None, ATT_TILE, gw), lambda bi, g, qi: (bi, qi, g)),
                   pl.BlockSpec((None, None, ATT_TILE, 128), lambda bi, g, qi: (bi, g, qi, 0))],
        out_shape=[jax.ShapeDtypeStruct((b, t, BRANCH_WIDTH), jnp.float32),
                   jax.ShapeDtypeStruct((b, NSA_GROUPS, t, 128), jnp.float32)],
        compiler_params=pltpu.CompilerParams(
            dimension_semantics=("parallel", "parallel", "parallel"), vmem_limit_bytes=VMEM_LIMIT),
    )(p3, kv_cmp, kv_cmp)


def _nsa_prompt_kernel(q_ref, ks_ref, vs_ref, kw_ref, vw_ref, sel_ref, ocmp_ref, ng_ref, bias_ref, o_ref,
                       keep_ref, s_ref, m_ref, l_ref, acc_ref, *, nblk):
    t = ATT_TILE
    g = pl.program_id(1)
    qi = pl.program_id(2)
    r, c = _tile_iotas()
    causal = _causal_tile()
    scale = HEAD_DIM ** -0.5
    per_tile = t // NSA_SEL_BLOCK
    sel = sel_ref[...]
    bb = lax.broadcasted_iota(jnp.int32, (128, t), 0)
    cc = lax.broadcasted_iota(jnp.int32, (128, t), 1)
    for n in range(nblk):
        @pl.when(n <= qi)
        def _(n=n):
            expand = (bb == per_tile * n + cc // NSA_SEL_BLOCK).astype(jnp.float32)
            keep_ref[:, n * t:(n + 1) * t] = jnp.dot(sel, expand, preferred_element_type=jnp.float32)
    gates = jax.nn.sigmoid(ng_ref[...])
    for j in range(NSA_HPG):
        q_bf = q_ref[:, j * HEAD_DIM:(j + 1) * HEAD_DIM].astype(jnp.bfloat16)

        def bias_of(n, j=j):
            return bias_ref[j, jnp.minimum(qi - n, N_BIAS_TILES - 1)]

        def sel_score(n, q_bf=q_bf, bias_of=bias_of):
            s = lax.dot_general(q_bf, ks_ref[n * t:(n + 1) * t, :].astype(jnp.bfloat16), _NT,
                                preferred_element_type=jnp.float32)
            s = s * scale + bias_of(n)
            keep = keep_ref[:, n * t:(n + 1) * t] * jnp.where(n < qi, 1.0, causal)
            return jnp.where(keep > 0.5, s, -jnp.inf)

        o_sel = _attend(nblk, lambda n: n <= qi, sel_score,
                        lambda n: vs_ref[n * t:(n + 1) * t, :].astype(jnp.bfloat16), s_ref, m_ref, l_ref, acc_ref)

        def win_score(n, q_bf=q_bf, bias_of=bias_of):
            s = lax.dot_general(q_bf, kw_ref[n * t:(n + 1) * t, :].astype(jnp.bfloat16), _NT,
                                preferred_element_type=jnp.float32)
            s = s * scale + bias_of(n)
            dist = (qi - n) * t + r - c
            return jnp.where((dist >= 0) & (dist < NSA_WINDOW), s, -jnp.inf)

        back = NSA_WINDOW // t
        o_win = _attend(nblk, lambda n: (n <= qi) & (n >= qi - back), win_score,
                        lambda n: vw_ref[n * t:(n + 1) * t, :].astype(jnp.bfloat16), s_ref, m_ref, l_ref, acc_ref)
        o_cmp = ocmp_ref[:, j * HEAD_DIM:(j + 1) * HEAD_DIM]
        gate_lane = lax.broadcasted_iota(jnp.int32, gates.shape, 1) - 3 * (g * NSA_HPG + j)
        g_cmp = jnp.sum(jnp.where(gate_lane == 0, gates, 0.0), axis=-1, keepdims=True)
        g_sel = jnp.sum(jnp.where(gate_lane == 1, gates, 0.0), axis=-1, keepdims=True)
        g_win = jnp.sum(jnp.where(gate_lane == 2, gates, 0.0), axis=-1, keepdims=True)
        o = g_cmp * o_cmp + g_sel * o_sel + g_win * o_win
        o_ref[:, j * HEAD_DIM:(j + 1) * HEAD_DIM] = o.astype(o_ref.dtype)


def nsa_prompt(p3, sel_mask, o_cmp, bias_tiles):
    b, t, _ = p3.shape
    nblk = t // ATT_TILE
    hd = HEAD_DIM
    gw = NSA_HPG * hd
    g_n = NSA_GROUPS

    def kv_spec(col):
        return pl.BlockSpec((None, t, hd), lambda bi, g, qi: (bi, 0, col // hd + g))

    return pl.pallas_call(
        functools.partial(_nsa_prompt_kernel, nblk=nblk),
        grid=(b, g_n, nblk),
        in_specs=[pl.BlockSpec((None, ATT_TILE, gw), lambda bi, g, qi: (bi, qi, C_NQ // gw + g)),
                  kv_spec(C_SK), kv_spec(C_SV), kv_spec(C_WK), kv_spec(C_WV),
                  pl.BlockSpec((None, None, ATT_TILE, 128), lambda bi, g, qi: (bi, g, qi, 0)),
                  pl.BlockSpec((None, ATT_TILE, gw), lambda bi, g, qi: (bi, qi, g)),
                  pl.BlockSpec((None, ATT_TILE, 128), lambda bi, g, qi: (bi, qi, C_NG // 128)),
                  pl.BlockSpec((None, NSA_HPG, N_BIAS_TILES, ATT_TILE, ATT_TILE),
                               lambda bi, g, qi: (g, 0, 0, 0, 0))],
        out_specs=pl.BlockSpec((None, ATT_TILE, gw), lambda bi, g, qi: (bi, qi, g)),
        out_shape=jax.ShapeDtypeStruct((b, t, BRANCH_WIDTH), jnp.bfloat16),
        scratch_shapes=[pltpu.VMEM((ATT_TILE, t), jnp.float32),
                        pltpu.VMEM((ATT_TILE, t), jnp.float32),
                        pltpu.VMEM((ATT_TILE, 1), jnp.float32),
                        pltpu.VMEM((ATT_TILE, 1), jnp.float32),
                        pltpu.VMEM((ATT_TILE, hd), jnp.float32)],
        compiler_params=pltpu.CompilerParams(
            dimension_semantics=("parallel", "parallel", "parallel"), vmem_limit_bytes=VMEM_LIMIT),
    )(p3, p3, p3, p3, p3, sel_mask, o_cmp, p3, bias_tiles)


def _page_softmax(s):
    m = jnp.max(s, axis=-1, keepdims=True)
    e = jnp.exp(s - jnp.where(m == -jnp.inf, 0.0, m))
    return m, e, jnp.sum(e, axis=-1, keepdims=True)


def _is_last(p, last_page):
    return (p == last_page).astype(jnp.int32)


PAGES_PER_STEP = 4


def _moba_pages_kernel(pt_ref, q_ref, *refs, last_page):
    kv_refs = refs[:PAGES_PER_STEP]
    bias_ref, o_ref, m_ref, l_ref, ks_ref = refs[PAGES_PER_STEP:]
    q = q_ref[...]
    for k, kv_ref in enumerate(kv_refs):
        p = pl.program_id(1) * PAGES_PER_STEP + k
        k3 = kv_ref[:, 0]
        v3 = kv_ref[:, 1]
        rows = k3.shape[0] * k3.shape[1]
        ks_ref[k] = jnp.sum(k3, axis=0)
        kx = k3.reshape(rows, HEAD_DIM).astype(jnp.bfloat16)
        vx = v3.reshape(rows, HEAD_DIM).astype(jnp.bfloat16)
        s = lax.dot_general(q, kx, _NT, preferred_element_type=jnp.float32)
        s = s * HEAD_DIM ** -0.5 + bias_ref[_is_last(p, last_page)]
        row = lax.broadcasted_iota(jnp.int32, s.shape, 0)
        lane = lax.broadcasted_iota(jnp.int32, s.shape, 1)
        s = jnp.where((lane & (MOBA_HEADS - 1)) == (row & (MOBA_HEADS - 1)), s, -jnp.inf)
        m, e, l = _page_softmax(s)
        o_ref[k] = jnp.dot(e.astype(jnp.bfloat16), vx, preferred_element_type=jnp.float32)
        m_ref[k] = m
        l_ref[k] = l


def _page_specs(block, index_of_page):
    return [pl.BlockSpec(block, functools.partial(
        lambda bi, i, pt, k: index_of_page(pt[bi, i * PAGES_PER_STEP + k]), k=k)) for k in range(PAGES_PER_STEP)]


def _page_stat_specs(n_rows, width):
    def im(bi, i, pt):
        return (bi, i, 0, 0)
    return [pl.BlockSpec((None, PAGES_PER_STEP, n_rows, width), im),
            pl.BlockSpec((None, PAGES_PER_STEP, n_rows, 1), im),
            pl.BlockSpec((None, PAGES_PER_STEP, n_rows, 1), im)]


def _page_stat_shapes(b, pages, n_rows, width):
    return [jax.ShapeDtypeStruct((b, pages, n_rows, width), jnp.float32),
            jax.ShapeDtypeStruct((b, pages, n_rows, 1), jnp.float32),
            jax.ShapeDtypeStruct((b, pages, n_rows, 1), jnp.float32)]


_PAGE_PARAMS = pltpu.CompilerParams(dimension_semantics=("parallel", "parallel"), vmem_limit_bytes=VMEM_LIMIT)


def moba_pages(q_rows, cache, layer, page_table, bias2):
    b, n_rows, _ = q_rows.shape
    pages = page_table.shape[1]
    page = cache.shape[2]
    assert MOBA_HEADS & (MOBA_HEADS - 1) == 0 and pages % PAGES_PER_STEP == 0
    grid_spec = pltpu.PrefetchScalarGridSpec(
        num_scalar_prefetch=1, grid=(b, pages // PAGES_PER_STEP),
        in_specs=[pl.BlockSpec((None, n_rows, HEAD_DIM), lambda bi, i, pt: (bi, 0, 0))]
        + _page_specs((None, None, page, 2, MOBA_HEADS, HEAD_DIM), lambda phys: (layer, phys, 0, 0, 0, 0))
        + [pl.BlockSpec((2, n_rows, page * MOBA_HEADS), lambda bi, i, pt: (0, 0, 0))],
        out_specs=_page_stat_specs(n_rows, HEAD_DIM) + [
            pl.BlockSpec((None, PAGES_PER_STEP, MOBA_HEADS, HEAD_DIM), lambda bi, i, pt: (bi, i, 0, 0))])
    return pl.pallas_call(
        functools.partial(_moba_pages_kernel, last_page=pages - 1),
        grid_spec=grid_spec,
        out_shape=_page_stat_shapes(b, pages, n_rows, HEAD_DIM) + [
            jax.ShapeDtypeStruct((b, pages, MOBA_HEADS, HEAD_DIM), jnp.float32)],
        compiler_params=_PAGE_PARAMS,
    )(page_table, q_rows, *([cache] * PAGES_PER_STEP), bias2)


def _diff_pages_kernel(pt_ref, q_ref, *refs, last_page):
    kv_refs = refs[:PAGES_PER_STEP]
    bias_ref, o_ref, m_ref, l_ref = refs[PAGES_PER_STEP:]
    q = q_ref[...]
    for k, kv_ref in enumerate(kv_refs):
        p = pl.program_id(1) * PAGES_PER_STEP + k
        half = kv_ref.shape[1] // 2
        s = lax.dot_general(q, kv_ref[:, :half], _NT, preferred_element_type=jnp.float32)
        s = s * HEAD_DIM ** -0.5 + bias_ref[_is_last(p, last_page)]
        m, e, l = _page_softmax(s)
        pv = jnp.dot(e.astype(jnp.bfloat16), kv_ref[:, half:], preferred_element_type=jnp.float32)
        rph = s.shape[0] // DIFF_HEADS
        for h in range(DIFF_HEADS):
            o_ref[k, h * rph:(h + 1) * rph, :] = pv[h * rph:(h + 1) * rph, h * DIFF_VDIM:(h + 1) * DIFF_VDIM]
        m_ref[k] = m
        l_ref[k] = l


def diff_pages(q_blk, cache2d, layer, page_table, bias2):
    b, n_rows, kw = q_blk.shape
    pages = page_table.shape[1]
    page = cache2d.shape[2]
    assert pages % PAGES_PER_STEP == 0
    grid_spec = pltpu.PrefetchScalarGridSpec(
        num_scalar_prefetch=1, grid=(b, pages // PAGES_PER_STEP),
        in_specs=[pl.BlockSpec((None, n_rows, kw), lambda bi, i, pt: (bi, 0, 0))]
        + _page_specs((None, None, page, 2 * kw), lambda phys: (layer, phys, 0, 0))
        + [pl.BlockSpec((2, n_rows, page), lambda bi, i, pt: (0, 0, 0))],
        out_specs=_page_stat_specs(n_rows, DIFF_VDIM))
    return pl.pallas_call(
        functools.partial(_diff_pages_kernel, last_page=pages - 1),
        grid_spec=grid_spec,
        out_shape=_page_stat_shapes(b, pages, n_rows, DIFF_VDIM),
        compiler_params=_PAGE_PARAMS,
    )(page_table, q_blk, *([cache2d] * PAGES_PER_STEP), bias2)


NSA_ROWS_PER_TOKEN = 4 * NSA_GROUPS


def _compress_pages_kernel(pt_ref, pg0_ref, pg1_ref, pe_ref, w1_ref, b1_ref, w2_ref, b2_ref, o_ref, cbuf_ref,
                           *, n_steps):
    i = pl.program_id(1)
    st = NSA_CMP_STRIDE
    rpt = NSA_ROWS_PER_TOKEN
    chunks = pg0_ref.shape[0] // (st * rpt)
    base = pl.multiple_of(i * 2 * chunks, 2 * chunks)
    for kg in range(2 * NSA_GROUPS):
        for l in range(st):
            x0 = pg0_ref[pl.ds(l * rpt + kg, chunks, stride=st * rpt), :]
            x1 = pg1_ref[pl.ds(l * rpt + kg, chunks, stride=st * rpt), :]
            cbuf_ref[kg, pl.ds(base, 2 * chunks), l * HEAD_DIM:(l + 1) * HEAD_DIM] = (
                jnp.concatenate([x0, x1], axis=0).astype(jnp.bfloat16))

    @pl.when(i == n_steps - 1)
    def _():
        n_ch = cbuf_ref.shape[1]
        half = st * HEAD_DIM
        for kind in range(2):
            const = jnp.dot(pe_ref[kind].astype(jnp.bfloat16), w1_ref[kind],
                            preferred_element_type=jnp.float32)[0:1] + b1_ref[kind]
            for g in range(NSA_GROUPS):
                c = cbuf_ref[kind * NSA_GROUPS + g]
                pre_a = jnp.dot(c, w1_ref[kind, :half, :], preferred_element_type=jnp.float32)
                pre_b = jnp.dot(c, w1_ref[kind, half:, :], preferred_element_type=jnp.float32)
                pre = pre_a + pltpu.roll(pre_b, n_ch - 1, 0) + const
                o_ref[kind, g] = jnp.dot(_gelu_tanh(pre).astype(jnp.bfloat16), w2_ref[kind],
                                         preferred_element_type=jnp.float32) + b2_ref[kind]


def compress_pages(cache2d, layer, page_table, cmp_pos, cmp_w1, cmp_b1, cmp_w2, cmp_b2):
    b, pages = page_table.shape
    rows = cache2d.shape[2]
    page = rows // NSA_ROWS_PER_TOKEN
    n_ch = pages * page // NSA_CMP_STRIDE
    feat = NSA_CMP_BLOCK * HEAD_DIM
    assert pages % 2 == 0 and (2 * page // NSA_CMP_STRIDE) % 16 == 0
    pe = jnp.broadcast_to(cmp_pos.reshape(2, 1, feat), (2, 8, feat))
    grid_spec = pltpu.PrefetchScalarGridSpec(
        num_scalar_prefetch=1, grid=(b, pages // 2),
        in_specs=[pl.BlockSpec((None, None, rows, HEAD_DIM), lambda bi, i, pt: (layer, pt[bi, 2 * i], 0, 0)),
                  pl.BlockSpec((None, None, rows, HEAD_DIM), lambda bi, i, pt: (layer, pt[bi, 2 * i + 1], 0, 0)),
                  pl.BlockSpec((2, 8, feat), lambda bi, i, pt: (0, 0, 0)),
                  pl.BlockSpec((2, feat, NSA_CMP_HIDDEN), lambda bi, i, pt: (0, 0, 0)),
                  pl.BlockSpec((2, 1, NSA_CMP_HIDDEN), lambda bi, i, pt: (0, 0, 0)),
                  pl.BlockSpec((2, NSA_CMP_HIDDEN, HEAD_DIM), lambda bi, i, pt: (0, 0, 0)),
                  pl.BlockSpec((2, 1, HEAD_DIM), lambda bi, i, pt: (0, 0, 0))],
        out_specs=pl.BlockSpec((None, 2, NSA_GROUPS, n_ch, HEAD_DIM), lambda bi, i, pt: (bi, 0, 0, 0, 0)),
        scratch_shapes=[pltpu.VMEM((2 * NSA_GROUPS, n_ch, NSA_CMP_STRIDE * HEAD_DIM), jnp.bfloat16)])
    return pl.pallas_call(
        functools.partial(_compress_pages_kernel, n_steps=pages // 2),
        grid_spec=grid_spec,
        out_shape=jax.ShapeDtypeStruct((b, 2, NSA_GROUPS, n_ch, HEAD_DIM), jnp.float32),
        compiler_params=pltpu.CompilerParams(
            dimension_semantics=("parallel", "arbitrary"), vmem_limit_bytes=VMEM_LIMIT),
    )(page_table, cache2d, cache2d, pe, cmp_w1.reshape(2, feat, NSA_CMP_HIDDEN).astype(jnp.bfloat16),
      cmp_b1.reshape(2, 1, NSA_CMP_HIDDEN), cmp_w2.astype(jnp.bfloat16), cmp_b2.reshape(2, 1, HEAD_DIM))


def _nsa_sel_pages_kernel(pt_ref, q_ref, *refs, last_page, group_shift):
    pg_refs = refs[:PAGES_PER_STEP]
    km_ref, bias_ref, o_ref, m_ref, l_ref = refs[PAGES_PER_STEP:]
    rpt = NSA_ROWS_PER_TOKEN
    q = q_ref[...]
    tokens = pg_refs[0].shape[0] // rpt
    n_lanes = pg_refs[0].shape[0]
    tok = lax.broadcasted_iota(jnp.int32, (tokens, n_lanes), 0)
    ln = lax.broadcasted_iota(jnp.int32, (tokens, n_lanes), 1)
    spread = ((ln >> 3) == tok).astype(jnp.float32)
    row = lax.broadcasted_iota(jnp.int32, (q.shape[0], n_lanes), 0)
    lane = lax.broadcasted_iota(jnp.int32, (q.shape[0], n_lanes), 1)
    own_rows = (lane & (rpt - 1)) == 2 * NSA_GROUPS + (row >> group_shift)
    for k, pg_ref in enumerate(pg_refs):
        p = pl.program_id(1) * PAGES_PER_STEP + k
        x = pg_ref[...].astype(jnp.bfloat16)
        s = lax.dot_general(q, x, _NT, preferred_element_type=jnp.float32)
        s = s * HEAD_DIM ** -0.5 + bias_ref[_is_last(p, last_page)]
        keep = jnp.dot(km_ref[:, k * tokens:(k + 1) * tokens], spread, preferred_element_type=jnp.float32)
        s = jnp.where((keep > 0.5) & own_rows, s, -jnp.inf)
        m, e, l = _page_softmax(s)
        o_ref[k] = jnp.dot(pltpu.roll(e, NSA_GROUPS, 1).astype(jnp.bfloat16), x, preferred_element_type=jnp.float32)
        m_ref[k] = m
        l_ref[k] = l


def nsa_sel_pages(q_rows, cache2d, layer, page_table, key_mask, bias2):
    b, n_rows, _ = q_rows.shape
    pages = page_table.shape[1]
    rows = cache2d.shape[2]
    page = rows // NSA_ROWS_PER_TOKEN
    rows_per_group = n_rows // NSA_GROUPS
    assert NSA_ROWS_PER_TOKEN == 8 and rows_per_group & (rows_per_group - 1) == 0 and pages % PAGES_PER_STEP == 0
    grid_spec = pltpu.PrefetchScalarGridSpec(
        num_scalar_prefetch=1, grid=(b, pages // PAGES_PER_STEP),
        in_specs=[pl.BlockSpec((None, n_rows, HEAD_DIM), lambda bi, i, pt: (bi, 0, 0))]
        + _page_specs((None, None, rows, HEAD_DIM), lambda phys: (layer, phys, 0, 0))
        + [pl.BlockSpec((None, n_rows, PAGES_PER_STEP * page), lambda bi, i, pt: (bi, 0, i)),
           pl.BlockSpec((2, n_rows, rows), lambda bi, i, pt: (0, 0, 0))],
        out_specs=_page_stat_specs(n_rows, HEAD_DIM))
    return pl.pallas_call(
        functools.partial(_nsa_sel_pages_kernel, last_page=pages - 1,
                          group_shift=rows_per_group.bit_length() - 1),
        grid_spec=grid_spec,
        out_shape=_page_stat_shapes(b, pages, n_rows, HEAD_DIM),
        compiler_params=_PAGE_PARAMS,
    )(page_table, q_rows, *([cache2d] * PAGES_PER_STEP), key_mask, bias2)


def _merge_pages(m_pg, l_pg, o_pg, w_pg, s_own, v_own):
    m_sel = jnp.where(w_pg, m_pg, -jnp.inf)
    top = jnp.maximum(jnp.max(m_sel, axis=-1), jnp.max(s_own, axis=-1))[..., None]
    wp = jnp.exp(m_sel - top)
    e_own = jnp.exp(s_own - top)
    den = jnp.sum(wp * l_pg, axis=-1) + jnp.sum(e_own, axis=-1)
    num = jnp.sum(wp[..., None] * o_pg, axis=-2) + jnp.sum(e_own[..., None] * v_own, axis=-2)
    return num / den[..., None]


def _decode_bias(strip, t, page):
    assert page >= REL_MAX_DIST
    n_ab = MOBA_HEADS + DIFF_HEADS
    d_last = jnp.clip(page + jnp.arange(t)[:, None] - jnp.arange(page)[None, :], 0, REL_MAX_DIST)
    last = strip[d_last]
    far = jnp.broadcast_to(strip[REL_MAX_DIST], last.shape)
    out = []
    for src in (far, last):
        a = jnp.transpose(src[:, :, :MOBA_HEADS], (0, 2, 1)).reshape(t * MOBA_HEADS, page)
        a = jnp.repeat(a, MOBA_HEADS, axis=1)
        bb = jnp.transpose(src[:, :, MOBA_HEADS:n_ab], (2, 0, 1))
        bb = jnp.broadcast_to(bb[:, None], (DIFF_HEADS, 2, t, page)).reshape(DIFF_HEADS * 2 * t, page)
        c = src[:, :, n_ab:].reshape(t, page, NSA_GROUPS, NSA_HPG)
        c = jnp.transpose(c, (2, 0, 3, 1)).reshape(NSA_GROUPS * t * NSA_HPG, page)
        c = jnp.repeat(c, NSA_ROWS_PER_TOKEN, axis=1)
        out.append((a, bb, c))
    return tuple(jnp.stack([out[0][i], out[1][i]]) for i in range(3))


def _masked_softmax(s, mask, axis):
    s = jnp.where(mask, s, -jnp.inf)
    m = jnp.max(s, axis=axis, keepdims=True)
    m = jnp.where(jnp.isfinite(m), m, 0.0)
    e = jnp.where(mask, jnp.exp(s - m), 0.0)
    return e / jnp.maximum(jnp.sum(e, axis=axis, keepdims=True), 1e-30)


def _topk_mask(score, n_top):
    nb = score.shape[-1]
    iota = lax.broadcasted_iota(jnp.int32, score.shape, score.ndim - 1)
    sel = jnp.zeros(score.shape, bool)
    for _ in range(n_top):
        m = jnp.max(score, axis=-1, keepdims=True)
        first = jnp.min(jnp.where(score == m, iota, nb), axis=-1, keepdims=True)
        hit = (iota == first) & (m > -jnp.inf)
        sel = sel | hit
        score = jnp.where(iota == first, -jnp.inf, score)
    return sel


def _bias_of_dist(strip, q_pos, k_pos):
    d = jnp.clip(q_pos[:, None] - k_pos[None, :], 0, REL_MAX_DIST)
    return strip[d]


def _own_logits(s, strip_h, t):
    tq = jnp.arange(t)
    dist = tq[:, None] - tq[None, :]
    s = s * HEAD_DIM ** -0.5 + strip_h[jnp.clip(dist, 0, REL_MAX_DIST)]
    return jnp.where((dist >= 0)[..., None], s, -jnp.inf)


def _decode_moba(p3, q_start, layer, cache, page_table, bias2, strip_a):
    b, t, _ = p3.shape
    h, d = MOBA_HEADS, HEAD_DIM
    pages = page_table.shape[1]
    page = cache.shape[2]
    ppb = MOBA_BLOCK // page
    assert q_start == pages * page and q_start % MOBA_BLOCK == 0 and t <= MOBA_BLOCK and MOBA_BLOCK % page == 0
    q = p3[:, :, C_MQ:C_MQ + _BW].reshape(b, t, h, d)
    k_new = p3[:, :, C_MK:C_MK + _BW].reshape(b, t, h, d)
    v_new = p3[:, :, C_MV:C_MV + _BW].reshape(b, t, h, d)
    o_pg, m_pg, l_pg, ksum = moba_pages(q.reshape(b, t * h, d).astype(jnp.bfloat16), cache, layer, page_table, bias2)
    k_mean = jnp.sum(ksum.reshape(b, pages // ppb, ppb, h, d), axis=2) / MOBA_BLOCK
    score = jnp.einsum('bthd,bnhd->bthn', q, k_mean, precision=lax.Precision.HIGHEST)
    sel = _topk_mask(score, min(MOBA_TOPK, q_start // MOBA_BLOCK))
    w_pg = jnp.repeat(sel, ppb, axis=-1)

    def per_query(a):
        return jnp.transpose(a.reshape(b, pages, t, h, a.shape[-1]), (0, 2, 3, 1, 4))

    s_own = _own_logits(jnp.einsum('bthd,bshd->btsh', q, k_new), strip_a, t)
    s_own = jnp.transpose(s_own, (0, 1, 3, 2))
    v_own = jnp.transpose(v_new, (0, 2, 1, 3))[:, None]
    o = _merge_pages(per_query(m_pg)[..., 0], per_query(l_pg)[..., 0], per_query(o_pg), w_pg, s_own, v_own)
    return o.reshape(b, t, _BW)


def _decode_diff(p3, q_start, layer, cache2d, page_table, bias2, strip_b, lam, subln_g, lam_init):
    b, t, _ = p3.shape
    h, d = DIFF_HEADS, HEAD_DIM
    pages = page_table.shape[1]
    assert q_start == pages * cache2d.shape[2]
    q = p3[:, :, C_DQ:C_DQ + _BW].reshape(b, t, h, 2, d)
    k_new = p3[:, :, C_DK:C_DK + _BW].reshape(b, t, h, 2, d)
    v_new = p3[:, :, C_DV:C_DV + _BW].reshape(b, t, h, DIFF_VDIM)
    q_rows = jnp.transpose(q, (0, 2, 3, 1, 4)).reshape(b, 2 * h, t, 1, d)
    q_blk = (q_rows * jnp.eye(2 * h, dtype=q.dtype)[None, :, None, :, None]).reshape(b, 2 * h * t, 2 * h * d)
    o_pg, m_pg, l_pg = diff_pages(q_blk.astype(jnp.bfloat16), cache2d, layer, page_table, bias2)

    def per_query(a):
        return jnp.transpose(a.reshape(b, pages, h, 2, t, a.shape[-1]), (0, 2, 3, 4, 1, 5))

    s_own = _own_logits(jnp.einsum('bthmd,bshmd->bmtsh', q, k_new), strip_b, t)
    s_own = jnp.transpose(s_own, (0, 4, 1, 2, 3))
    v_own = jnp.transpose(v_new, (0, 2, 1, 3))[:, :, None, None]
    w_pg = jnp.ones((b, h, 2, t, pages), bool)
    o = _merge_pages(per_query(m_pg)[..., 0], per_query(l_pg)[..., 0], per_query(o_pg), w_pg, s_own, v_own)
    o = jnp.transpose(o[:, :, 0] - lam * o[:, :, 1], (0, 2, 1, 3))
    o = o * lax.rsqrt(jnp.mean(o * o, axis=-1, keepdims=True) + SUBLN_EPS) * subln_g * (1.0 - lam_init)
    return o.reshape(b, t, _BW)


def _cmp_attention_select(q, q_start, k_cmp, v_cmp, l):
    b, t, g, j, d = q.shape
    n_cmp = k_cmp.shape[2]
    q_pos = q_start + jnp.arange(t)
    cmp_end = jnp.arange(n_cmp) * NSA_CMP_STRIDE + NSA_CMP_BLOCK - 1
    valid = cmp_end[None, :] <= q_pos[:, None]
    s = jnp.einsum('btgjd,bgnd->bgjtn', q, k_cmp, precision=lax.Precision.HIGHEST) * d ** -0.5
    p = _masked_softmax(s, valid, -1)
    o_cmp = jnp.einsum('bgjtn,bgnd->btgjd', p, v_cmp)
    imp = jnp.sum(p, axis=2)
    nbs = -(-l // NSA_SEL_BLOCK)
    ratio = NSA_SEL_BLOCK // NSA_CMP_STRIDE
    front = NSA_CMP_BLOCK // NSA_CMP_STRIDE - 1
    span = ratio + front
    back = max(ratio * (nbs - 1) + span - front - n_cmp, 0)
    imp = jnp.pad(imp, ((0, 0), (0, 0), (0, 0), (front, back)))
    slc = sum(imp[..., o:o + ratio * nbs:ratio] for o in range(span))
    slc = jnp.transpose(slc, (0, 2, 1, 3))
    own = q_pos // NSA_SEL_BLOCK
    blk = jnp.arange(nbs)
    elig = blk[None, :] < own[:, None]
    forced = (blk[None, :] == 0) | (blk[None, :] == own[:, None] - 1)
    score = jnp.where(forced[None, :, None, :], jnp.inf, slc)
    score = jnp.where(elig[None, :, None, :], score, -jnp.inf)
    n_top = min(NSA_SEL_COUNT - 1, (q_start + t - 1) // NSA_SEL_BLOCK)
    sel = _topk_mask(score, n_top) | (blk[None, :] == own[:, None])[None, :, None, :]
    return o_cmp, sel


def _decode_nsa(p3, q_start, layer, cache2d, page_table, bias2, strip_c, win_buf, cmp_w):
    b, t, _ = p3.shape
    g, j, d = NSA_GROUPS, NSA_HPG, HEAD_DIM
    pages = page_table.shape[1]
    page = cache2d.shape[2] // NSA_ROWS_PER_TOKEN
    past = pages * page
    assert q_start == past and past % NSA_SEL_BLOCK == 0 and t <= NSA_SEL_BLOCK and (q_start + t) // NSA_CMP_STRIDE == past // NSA_CMP_STRIDE
    q = p3[:, :, C_NQ:C_NQ + _BW].reshape(b, t, g, j, d)
    kv_cmp = compress_pages(cache2d, layer, page_table, *cmp_w)
    n_cmp = past // NSA_CMP_STRIDE - NSA_CMP_BLOCK // NSA_CMP_STRIDE + 1
    o_cmp, sel = _cmp_attention_select(q, q_start, kv_cmp[:, 0, :, :n_cmp], kv_cmp[:, 1, :, :n_cmp], q_start + t)
    key_mask = jnp.repeat(sel[..., :past // NSA_SEL_BLOCK], NSA_SEL_BLOCK, axis=-1)
    key_mask = jnp.broadcast_to(jnp.transpose(key_mask, (0, 2, 1, 3))[:, :, :, None], (b, g, t, j, past))
    q_rows = jnp.transpose(q, (0, 2, 1, 3, 4)).reshape(b, g * t * j, d)
    o_pg, m_pg, l_pg = nsa_sel_pages(q_rows.astype(jnp.bfloat16), cache2d, layer, page_table,
                                     key_mask.reshape(b, g * t * j, past).astype(jnp.float32), bias2)

    def per_query(a):
        return jnp.transpose(a.reshape(b, pages, g, t, j, a.shape[-1]), (0, 2, 3, 4, 1, 5))

    sk_new = p3[:, :, C_SK:C_SK + _GW].reshape(b, t, g, d)
    sv_new = p3[:, :, C_SV:C_SV + _GW].reshape(b, t, g, d)
    s_own = _own_logits(jnp.einsum('btgjd,bsgd->btsgj', q, sk_new).reshape(b, t, t, g * j), strip_c, t)
    s_own = jnp.transpose(s_own.reshape(b, t, t, g, j), (0, 3, 1, 4, 2))
    v_own = jnp.transpose(sv_new, (0, 2, 1, 3))[:, :, None, None]
    w_pg = jnp.ones((b, g, t, j, pages), bool)
    o_sel = _merge_pages(per_query(m_pg)[..., 0], per_query(l_pg)[..., 0], per_query(o_pg), w_pg, s_own, v_own)
    o_sel = jnp.transpose(o_sel, (0, 2, 1, 3, 4))

    win_rows = p3[:, :, C_WK:C_WK + 2 * _GW].reshape(b, t, 2, g, d)
    w = win_buf.shape[1]
    kv = jnp.concatenate([win_buf, win_rows], axis=1)
    o_win = window_attention(q, q_start + jnp.arange(t), kv[:, :, 0], kv[:, :, 1],
                             q_start - w + jnp.arange(w + t), strip_c)
    g_nsa = jax.nn.sigmoid(p3[:, :, C_NG:C_NG + N_NG].reshape(b, t, g, j, 3))
    o_c = g_nsa[..., 0:1] * o_cmp + g_nsa[..., 1:2] * o_sel + g_nsa[..., 2:3] * o_win
    return o_c.reshape(b, t, _BW), kv[:, t:]


def window_attention(q, q_pos, k, v, k_pos, strip):
    b, t, g, j, d = q.shape
    l = k.shape[1]
    dist = q_pos[:, None] - k_pos[None, :]
    mask = (k_pos[None, :] >= 0) & (dist >= 0) & (dist < NSA_WINDOW)
    bias = jnp.transpose(_bias_of_dist(strip, q_pos, k_pos).reshape(t, l, g, j), (2, 3, 0, 1))
    s = jnp.einsum('btgjd,bsgd->bgjts', q, k) * d ** -0.5 + bias[None]
    p = _masked_softmax(s, mask, -1)
    return jnp.einsum('bgjts,bsgd->btgjd', p, v)


def _layer(x, q_start, past, win_buf, conv_buf, bias, lam_init, wts):
    (g_attn, w_in_p, lam_p, subln_g, cmp_pos, cmp_w1, cmp_b1, cmp_w2, cmp_b2,
     w_branch, w_out, g_ffn, w_gate, w_up, w_down, conv_w, conv_b) = wts
    b, t, _ = x.shape
    m = b * t
    hd = HEAD_DIM
    x2 = x.reshape(m, D_MODEL)
    h = rmsnorm_rows(x2, g_attn, RMS_EPS, jnp.bfloat16)
    proj = matmul(h, w_in_p)
    p3 = proj.reshape(b, t, PROJ_PACKED)

    def cols(start, width):
        return p3[:, :, start:start + width]

    moba_rows = cols(C_MK, 2 * _BW).reshape(b, t, 2, MOBA_HEADS, hd)
    diff_rows = cols(C_DK, 2 * _BW).reshape(b, t, 2, DIFF_HEADS, DIFF_VDIM)
    nsa_rows = cols(C_CK, 4 * _GW).reshape(b, t, 4, NSA_GROUPS, hd)
    win_rows = cols(C_WK, 2 * _GW).reshape(b, t, 2, NSA_GROUPS, hd)
    strip, tiles = bias
    n_ab = MOBA_HEADS + DIFF_HEADS
    lam_f = lam_p.astype(jnp.float32)
    lam = jnp.exp(jnp.sum(lam_f[0] * lam_f[1])) - jnp.exp(jnp.sum(lam_f[2] * lam_f[3])) + lam_init
    if past is None:
        o_a = moba_prompt(p3, tiles[:MOBA_HEADS])
        o_b = diff_prompt(p3, tiles[MOBA_HEADS:n_ab], lam, subln_g, lam_init)
        kv_cmp = compress_prompt(p3, cmp_pos, cmp_w1, cmp_b1, cmp_w2, cmp_b2)
        o_cmp, sel_mask = cmp_select_prompt(p3, kv_cmp)
        o_c = nsa_prompt(p3, sel_mask, o_cmp,
                         tiles[n_ab:].reshape(NSA_GROUPS, NSA_HPG, N_BIAS_TILES, ATT_TILE, ATT_TILE))
        new_win = win_rows[:, t - min(NSA_WINDOW, t):]
    else:
        layer, (cache_a, cache_b, cache_c), page_table, (bias_a, bias_b, bias_c) = past
        o_a = _decode_moba(p3, q_start, layer, cache_a, page_table, bias_a, strip[:, :MOBA_HEADS])
        o_b = _decode_diff(p3, q_start, layer, cache_b, page_table, bias_b, strip[:, MOBA_HEADS:n_ab],
                           lam, subln_g, lam_init)
        o_c, new_win = _decode_nsa(p3, q_start, layer, cache_c, page_table, bias_c, strip[:, n_ab:], win_buf,
                                   (cmp_pos, cmp_w1, cmp_b1, cmp_w2, cmp_b2))
    branches = [o.reshape(m, _BW).astype(jnp.bfloat16) for o in (o_a, o_b, o_c)]
    mix = merge_branches(branches, w_branch, proj)
    x1 = matmul(mix, w_out, x2)

    hf = rmsnorm_rows(x1, g_ffn, RMS_EPS, jnp.bfloat16)
    if conv_buf is None:
        tm = 512
        hprev = hf.reshape(b, t // tm, tm, D_MODEL)[:, :, tm - 8:, :]
        hprev = jnp.concatenate([jnp.zeros_like(hprev[:, :1]), hprev[:, :-1]], axis=1)
        gprev = matmul(hprev.reshape(m // tm * 8, D_MODEL), w_gate).reshape(m // tm, 8, D_FF_PAD)
        act = ffn_in(hf, gprev, w_gate, w_up, conv_w, conv_b, tm=tm)
        g_last = matmul(hf.reshape(b, t, D_MODEL)[:, t - 8:].reshape(b * 8, D_MODEL), w_gate)
        new_conv = g_last.reshape(b, 8, D_FF_PAD)[:, 8 - (CONV_WIDTH - 1):, :D_FF]
    else:
        g = matmul(hf, w_gate).reshape(b, t, D_FF_PAD)
        u = matmul(hf, w_up)
        prev = jnp.pad(conv_buf, ((0, 0), (0, 0), (0, D_FF_PAD - D_FF)))
        gp = jnp.concatenate([prev, g], axis=1)
        shifted = [gp[:, i:i + t].reshape(m, D_FF_PAD) for i in range(CONV_WIDTH)]
        act = conv_act(*shifted, u, conv_w, conv_b)
        new_conv = gp[:, t:, :D_FF]
    x2o = matmul(act, w_down, x1, tm=1024, tn=1024, tk=2816)
    return x2o.reshape(b, t, D_MODEL), moba_rows, diff_rows, nsa_rows, new_win, new_conv


def _page_views(cache_moba_kv, cache_diff_kv, cache_nsa_kv):
    dd, n_pool, page = cache_diff_kv.shape[:3]
    diff2d = cache_diff_kv.reshape(dd, n_pool, page, -1).astype(jnp.bfloat16)
    nd = cache_nsa_kv.shape[0]
    nsa2d = cache_nsa_kv.reshape(nd, cache_nsa_kv.shape[1], page * NSA_ROWS_PER_TOKEN, HEAD_DIM)
    return cache_moba_kv, diff2d, nsa2d


def _pack_w_in(w_in):
    head = w_in[:, :, :C_BG_SRC]
    head = jnp.pad(head, ((0, 0), (0, 0), (0, NG_PAD - N_NG)))
    return jnp.concatenate([head, w_in[:, :, C_BG_SRC:]], axis=2).astype(jnp.bfloat16)


def kernel(x_prompt, x_sample, cache_moba_kv, cache_diff_kv, cache_nsa_kv, state_nsa_win, state_ffn_conv, page_table, rel_bias, norm_attn, w_in, diff_lambda, diff_subln, nsa_cmp_pos, nsa_cmp_w1, nsa_cmp_b1, nsa_cmp_w2, nsa_cmp_b2, w_branch, w_out, norm_ffn, w_gate, w_up, w_down, conv_w, conv_b, norm_final):
    depth = w_in.shape[0]
    past_len = page_table.shape[1] * cache_moba_kv.shape[2]
    bf = jnp.bfloat16
    fpad = D_FF_PAD - D_FF
    w_in_p = _pack_w_in(w_in)
    w_branch_b = w_branch.astype(bf)
    w_out_b = w_out.astype(bf)
    w_gate_b = jnp.pad(w_gate, ((0, 0), (0, 0), (0, fpad))).astype(bf)
    w_up_b = jnp.pad(w_up, ((0, 0), (0, 0), (0, fpad))).astype(bf)
    w_down_b = jnp.pad(w_down, ((0, 0), (0, fpad), (0, 0))).astype(bf)
    conv_w_p = jnp.pad(conv_w, ((0, 0), (0, 0), (0, fpad)))
    conv_b_p = jnp.pad(conv_b, ((0, 0), (0, fpad))).reshape(depth, 1, D_FF_PAD)
    strip = rel_bias[jnp.asarray(_BUCKET)]
    bias = (strip, _bias_tiles(strip))
    caches = _page_views(cache_moba_kv, cache_diff_kv, cache_nsa_kv)
    dec_bias = _decode_bias(strip, x_sample.shape[1], cache_moba_kv.shape[2])

    y_p, y_s = x_prompt, x_sample
    outs_p = [[] for _ in range(5)]
    outs_s = [[] for _ in range(5)]
    for layer in range(depth):
        lam_init = 0.8 - 0.6 * math.exp(-0.3 * layer)
        wts = (norm_attn[layer], w_in_p[layer], diff_lambda[layer], diff_subln[layer], nsa_cmp_pos[layer],
               nsa_cmp_w1[layer], nsa_cmp_b1[layer], nsa_cmp_w2[layer], nsa_cmp_b2[layer], w_branch_b[layer],
               w_out_b[layer], norm_ffn[layer], w_gate_b[layer], w_up_b[layer], w_down_b[layer],
               conv_w_p[layer], conv_b_p[layer])
        y_p, *new_p = _layer(y_p, 0, None, None, None, bias, lam_init, wts)

        past = (layer, caches, page_table, dec_bias)
        y_s, *new_s = _layer(y_s, past_len, past, state_nsa_win[layer], state_ffn_conv[layer], bias, lam_init, wts)
        for lst, a in zip(outs_p, new_p):
            lst.append(a)
        for lst, a in zip(outs_s, new_s):
            lst.append(a)
    bp, tp, _ = y_p.shape
    bs, ts, _ = y_s.shape
    y_p = rmsnorm_rows(y_p.reshape(bp * tp, D_MODEL), norm_final, RMS_EPS, jnp.float32).reshape(bp, tp, D_MODEL)
    y_s = rmsnorm_rows(y_s.reshape(bs * ts, D_MODEL), norm_final, RMS_EPS, jnp.float32).reshape(bs, ts, D_MODEL)
    sp = [jnp.stack(l) for l in outs_p]
    ss = [jnp.stack(l) for l in outs_s]
    return (y_p, y_s, sp[0], ss[0], sp[1], ss[1], sp[2], ss[2], sp[3], ss[3], sp[4], ss[4])
```

```python
import functools
import math

import numpy as np
import jax
import jax.numpy as jnp
from jax import lax
from jax.experimental import pallas as pl
from jax.experimental.pallas import tpu as pltpu

D_MODEL = 4096
HEAD_DIM = 128
MOBA_HEADS = 8
MOBA_BLOCK = 256
MOBA_TOPK = 3
DIFF_HEADS = 4
DIFF_VDIM = 2 * HEAD_DIM
NSA_HEADS = 8
NSA_GROUPS = 2
NSA_HPG = NSA_HEADS // NSA_GROUPS
NSA_CMP_BLOCK = 32
NSA_CMP_STRIDE = 16
NSA_CMP_HIDDEN = 2 * HEAD_DIM
NSA_SEL_BLOCK = 64
NSA_SEL_COUNT = 16
NSA_WINDOW = 512
REL_BUCKETS = 32
REL_MAX_DIST = 128
N_BRANCH = 3
BRANCH_WIDTH = MOBA_HEADS * HEAD_DIM
D_FF = 11008
CONV_WIDTH = 3
RMS_EPS = 1e-6
SUBLN_EPS = 1e-5

_BW = BRANCH_WIDTH
_GW = NSA_GROUPS * HEAD_DIM
C_MQ, C_MK, C_MV = 0, _BW, 2 * _BW
C_DQ, C_DK, C_DV = 3 * _BW, 4 * _BW, 5 * _BW
C_NQ = 6 * _BW
C_CK = 7 * _BW
C_CV, C_SK, C_SV, C_WK, C_WV = (C_CK + _GW, C_CK + 2 * _GW, C_CK + 3 * _GW,
                                C_CK + 4 * _GW, C_CK + 5 * _GW)
C_NG = C_CK + 6 * _GW
N_NG = 3 * NSA_HEADS
C_BG_SRC = C_NG + N_NG
NG_PAD = 512
C_BG = C_NG + NG_PAD
PROJ_PACKED = C_BG + N_BRANCH * D_MODEL
D_FF_PAD = 11264

VMEM_LIMIT = 56 * 1024 * 1024


def _bucket_of_dist():
    n = np.arange(REL_MAX_DIST + 1)
    exact = REL_BUCKETS // 2
    nf = np.maximum(n, 1).astype(np.float64)
    large = exact + (np.log(nf / exact) / math.log(REL_MAX_DIST / exact) * (REL_BUCKETS - exact)).astype(np.int64)
    return np.where(n < exact, n, np.minimum(large, REL_BUCKETS - 1)).astype(np.int32)


_BUCKET = _bucket_of_dist()


def _rmsnorm_kernel(x_ref, g_ref, o_ref, *, eps):
    x = x_ref[...]
    ms = jnp.mean(x * x, axis=-1, keepdims=True)
    o_ref[...] = (x * lax.rsqrt(ms + eps) * g_ref[...]).astype(o_ref.dtype)


def rmsnorm_rows(x, g, eps, out_dtype):
    m, d = x.shape
    tm = min(m, 256)
    return pl.pallas_call(
        functools.partial(_rmsnorm_kernel, eps=eps),
        grid=(m // tm,),
        in_specs=[pl.BlockSpec((tm, d), lambda i: (i, 0)),
                  pl.BlockSpec((1, d), lambda i: (0, 0))],
        out_specs=pl.BlockSpec((tm, d), lambda i: (i, 0)),
        out_shape=jax.ShapeDtypeStruct((m, d), out_dtype),
        compiler_params=pltpu.CompilerParams(dimension_semantics=("parallel",)),
    )(x, g.reshape(1, d).astype(jnp.float32))


def _matmul_kernel(a_ref, w_ref, *rest, nk, has_res):
    if has_res:
        r_ref, o_ref = rest
    else:
        (o_ref,) = rest
    part = jnp.dot(a_ref[...], w_ref[...], preferred_element_type=jnp.float32)
    if nk == 1:
        o_ref[...] = part + r_ref[...] if has_res else part
        return
    k = pl.program_id(2)

    @pl.when(k == 0)
    def _():
        o_ref[...] = part + r_ref[...] if has_res else part

    @pl.when(k > 0)
    def _():
        o_ref[...] += part


def matmul(a, w, res=None, *, tm=512, tn=1024, tk=None):
    m, kd = a.shape
    n = w.shape[1]
    tm = min(tm, m)
    tn = min(tn, n)
    tk = kd if tk is None else tk
    nk = kd // tk
    in_specs = [pl.BlockSpec((tm, tk), lambda j, i, k: (i, k)),
                pl.BlockSpec((tk, tn), lambda j, i, k: (k, j))]
    args = [a, w]
    if res is not None:
        in_specs.append(pl.BlockSpec((tm, tn), lambda j, i, k: (i, j)))
        args.append(res)
    return pl.pallas_call(
        functools.partial(_matmul_kernel, nk=nk, has_res=res is not None),
        grid=(n // tn, m // tm, nk),
        in_specs=in_specs,
        out_specs=pl.BlockSpec((tm, tn), lambda j, i, k: (i, j)),
        out_shape=jax.ShapeDtypeStruct((m, n), jnp.float32),
        compiler_params=pltpu.CompilerParams(
            dimension_semantics=("parallel", "parallel", "arbitrary"),
            vmem_limit_bytes=VMEM_LIMIT),
    )(*args)


def _merge_kernel(b0_ref, b1_ref, b2_ref, wb_ref, g0_ref, g1_ref, g2_ref, o_ref):
    acc = None
    for m, (b_ref, g_ref) in enumerate(((b0_ref, g0_ref), (b1_ref, g1_ref), (b2_ref, g2_ref))):
        u = jnp.dot(b_ref[...], wb_ref[m], preferred_element_type=jnp.float32)
        t = jax.nn.sigmoid(g_ref[...]) * u
        acc = t if acc is None else acc + t
    o_ref[...] = acc.astype(o_ref.dtype)


def merge_branches(branches, w_branch, proj, *, tm=512, tn=1024):
    m, bw = branches[0].shape
    d = w_branch.shape[2]
    tm = min(tm, m)
    gate_specs = [
        pl.BlockSpec((tm, tn), functools.partial(
            lambda j, i, base: (i, base + j), base=(C_BG + mm * d) // tn))
        for mm in range(N_BRANCH)]
    branch_spec = pl.BlockSpec((tm, bw), lambda j, i: (i, 0))
    return pl.pallas_call(
        _merge_kernel,
        grid=(d // tn, m // tm),
        in_specs=[branch_spec] * N_BRANCH + [pl.BlockSpec((N_BRANCH, bw, tn), lambda j, i: (0, 0, j))] + gate_specs,
        out_specs=pl.BlockSpec((tm, tn), lambda j, i: (i, j)),
        out_shape=jax.ShapeDtypeStruct((m, d), jnp.bfloat16),
        compiler_params=pltpu.CompilerParams(
            dimension_semantics=("parallel", "parallel"), vmem_limit_bytes=VMEM_LIMIT),
    )(*branches, w_branch, proj, proj, proj)


def _ffn_in_kernel(h_ref, gprev_ref, wg_ref, wu_ref, cw_ref, cb_ref, o_ref):
    h = h_ref[...]
    g = jnp.dot(h, wg_ref[...], preferred_element_type=jnp.float32)
    u = jnp.dot(h, wu_ref[...], preferred_element_type=jnp.float32)
    tm = g.shape[0]
    gcat = jnp.concatenate([gprev_ref[0], g], axis=0)
    cw = cw_ref[...]
    conv = (gcat[6:6 + tm] * cw[0:1] + gcat[7:7 + tm] * cw[1:2] + g * cw[2:3] + cb_ref[...])
    o_ref[...] = (jax.nn.silu(conv) * u).astype(o_ref.dtype)


def ffn_in(h, gprev, wg, wu, conv_w, conv_b, *, tm=512, tf=512):
    m, d = h.shape
    f = wg.shape[1]
    return pl.pallas_call(
        _ffn_in_kernel,
        grid=(f // tf, m // tm),
        in_specs=[pl.BlockSpec((tm, d), lambda j, i: (i, 0)),
                  pl.BlockSpec((1, 8, tf), lambda j, i: (i, 0, j)),
                  pl.BlockSpec((d, tf), lambda j, i: (0, j)),
                  pl.BlockSpec((d, tf), lambda j, i: (0, j)),
                  pl.BlockSpec((CONV_WIDTH, tf), lambda j, i: (0, j)),
                  pl.BlockSpec((1, tf), lambda j, i: (0, j))],
        out_specs=pl.BlockSpec((tm, tf), lambda j, i: (i, j)),
        out_shape=jax.ShapeDtypeStruct((m, f), jnp.bfloat16),
        compiler_params=pltpu.CompilerParams(
            dimension_semantics=("parallel", "parallel"), vmem_limit_bytes=VMEM_LIMIT),
    )(h, gprev, wg, wu, conv_w, conv_b)


def _conv_act_kernel(g0_ref, g1_ref, g2_ref, u_ref, cw_ref, cb_ref, o_ref):
    cw = cw_ref[...]
    conv = g0_ref[...] * cw[0:1] + g1_ref[...] * cw[1:2] + g2_ref[...] * cw[2:3] + cb_ref[...]
    o_ref[...] = (jax.nn.silu(conv) * u_ref[...]).astype(o_ref.dtype)


def conv_act(g0, g1, g2, u, conv_w, conv_b, *, tf=1024):
    m, f = u.shape
    row = pl.BlockSpec((m, tf), lambda j: (0, j))
    return pl.pallas_call(
        _conv_act_kernel,
        grid=(f // tf,),
        in_specs=[row, row, row, row,
                  pl.BlockSpec((CONV_WIDTH, tf), lambda j: (0, j)),
                  pl.BlockSpec((1, tf), lambda j: (0, j))],
        out_specs=row,
        out_shape=jax.ShapeDtypeStruct((m, f), jnp.bfloat16),
        compiler_params=pltpu.CompilerParams(dimension_semantics=("parallel",)),
    )(g0, g1, g2, u, conv_w, conv_b)


ATT_TILE = 256
N_BIAS_TILES = 3
_HI = lax.Precision.HIGHEST
_NT = (((1,), (1,)), ((), ()))


def _bias_tiles(strip):
    t = ATT_TILE
    n = 2 * t
    k = np.arange(n)
    k = np.where(k < t, k, k - n)
    d = np.stack([np.clip(o * t - k, 0, REL_MAX_DIST) for o in range(N_BIAS_TILES)])
    v = jnp.transpose(strip[jnp.asarray(d)], (2, 0, 1))
    rows = jnp.tile(v, (1, 1, t))[:, :, :t * (n - 1)].reshape(v.shape[0], N_BIAS_TILES, t, n - 1)
    return rows[:, :, :, :t]


def _tile_iotas():
    r = lax.broadcasted_iota(jnp.int32, (ATT_TILE, ATT_TILE), 0)
    c = lax.broadcasted_iota(jnp.int32, (ATT_TILE, ATT_TILE), 1)
    return r, c


def _causal_tile():
    r, c = _tile_iotas()
    return (r >= c).astype(jnp.float32)


def _attend(nblk, cond, score_fn, v_fn, s_ref, m_ref, l_ref, acc_ref):
    t = ATT_TILE
    m_ref[...] = jnp.full(m_ref.shape, -jnp.inf, jnp.float32)
    for n in range(nblk):
        @pl.when(cond(n))
        def _(n=n):
            s = score_fn(n)
            s_ref[:, n * t:(n + 1) * t] = s
            m_ref[...] = jnp.maximum(m_ref[...], jnp.max(s, axis=-1, keepdims=True))
    m = m_ref[...]
    m = jnp.where(m == -jnp.inf, 0.0, m)
    l_ref[...] = jnp.zeros(l_ref.shape, jnp.float32)
    acc_ref[...] = jnp.zeros(acc_ref.shape, jnp.float32)
    for n in range(nblk):
        @pl.when(cond(n))
        def _(n=n):
            e = jnp.exp(s_ref[:, n * t:(n + 1) * t] - m)
            l_ref[...] += jnp.sum(e, axis=-1, keepdims=True)
            acc_ref[...] += jnp.dot(e.astype(jnp.bfloat16), v_fn(n), preferred_element_type=jnp.float32)
    return acc_ref[...] / jnp.maximum(l_ref[...], 1e-30)


def _select_top(score, n_top):
    lane = lax.broadcasted_iota(jnp.int32, score.shape, 1)
    sel = jnp.zeros(score.shape, jnp.float32)
    for _ in range(n_top):
        m = jnp.max(score, axis=-1, keepdims=True)
        first = jnp.min(jnp.where(score == m, lane, score.shape[1]), axis=-1, keepdims=True)
        hit = lane == first
        sel = jnp.where(hit & (m > -jnp.inf), 1.0, sel)
        score = jnp.where(hit, -jnp.inf, score)
    return sel


def _moba_prompt_kernel(q_ref, k_ref, v_ref, bias_ref, o_ref, kmean_ref, s_ref, m_ref, l_ref, acc_ref, *, nblk):
    t = ATT_TILE
    qi = pl.program_id(2)

    @pl.when(qi == 0)
    def _():
        kmean_ref[...] = jnp.zeros(kmean_ref.shape, jnp.float32)
        for n in range(nblk):
            kmean_ref[n:n + 1, :] = jnp.mean(k_ref[n * t:(n + 1) * t, :], axis=0, keepdims=True)

    q = q_ref[...]
    lane = lax.broadcasted_iota(jnp.int32, (t, HEAD_DIM), 1)
    score = lax.dot_general(q, kmean_ref[...], _NT, precision=_HI, preferred_element_type=jnp.float32)
    score = jnp.where(lane < qi, score, -jnp.inf)
    sel = _select_top(score, min(MOBA_TOPK, nblk - 1))
    q_bf = q.astype(jnp.bfloat16)
    causal = _causal_tile()
    scale = HEAD_DIM ** -0.5

    def score_fn(n):
        s = lax.dot_general(q_bf, k_ref[n * t:(n + 1) * t, :].astype(jnp.bfloat16), _NT,
                            preferred_element_type=jnp.float32)
        s = s * scale + bias_ref[jnp.minimum(qi - n, N_BIAS_TILES - 1)]
        keep = jnp.where(n == qi, causal, jnp.broadcast_to(sel[:, n:n + 1], (t, t)))
        return jnp.where(keep > 0.5, s, -jnp.inf)

    o = _attend(nblk, lambda n: n <= qi, score_fn,
                lambda n: v_ref[n * t:(n + 1) * t, :].astype(jnp.bfloat16), s_ref, m_ref, l_ref, acc_ref)
    o_ref[...] = o.astype(o_ref.dtype)


def moba_prompt(p3, bias_tiles):
    b, t, _ = p3.shape
    nblk = t // ATT_TILE
    hd = HEAD_DIM
    return pl.pallas_call(
        functools.partial(_moba_prompt_kernel, nblk=nblk),
        grid=(b, MOBA_HEADS, nblk),
        in_specs=[pl.BlockSpec((None, ATT_TILE, hd), lambda bi, h, qi: (bi, qi, C_MQ // hd + h)),
                  pl.BlockSpec((None, t, hd), lambda bi, h, qi: (bi, 0, C_MK // hd + h)),
                  pl.BlockSpec((None, t, hd), lambda bi, h, qi: (bi, 0, C_MV // hd + h)),
                  pl.BlockSpec((None, N_BIAS_TILES, ATT_TILE, ATT_TILE), lambda bi, h, qi: (h, 0, 0, 0))],
        out_specs=pl.BlockSpec((None, ATT_TILE, hd), lambda bi, h, qi: (bi, qi, h)),
        out_shape=jax.ShapeDtypeStruct((b, t, BRANCH_WIDTH), jnp.bfloat16),
        scratch_shapes=[pltpu.VMEM((hd, hd), jnp.float32),
                        pltpu.VMEM((ATT_TILE, t), jnp.float32),
                        pltpu.VMEM((ATT_TILE, 1), jnp.float32),
                        pltpu.VMEM((ATT_TILE, 1), jnp.float32),
                        pltpu.VMEM((ATT_TILE, hd), jnp.float32)],
        compiler_params=pltpu.CompilerParams(
            dimension_semantics=("parallel", "parallel", "arbitrary"), vmem_limit_bytes=VMEM_LIMIT),
    )(p3, p3, p3, bias_tiles)


def _diff_prompt_kernel(lam_ref, q_ref, k_ref, v_ref, bias_ref, g_ref, o_ref, s_ref, m_ref, l_ref, acc_ref,
                        *, nblk, out_scale):
    t = ATT_TILE
    qi = pl.program_id(2)
    causal = _causal_tile()
    scale = HEAD_DIM ** -0.5
    outs = []
    for mp in range(2):
        q_bf = q_ref[:, mp * HEAD_DIM:(mp + 1) * HEAD_DIM].astype(jnp.bfloat16)

        def score_fn(n, mp=mp, q_bf=q_bf):
            kb = k_ref[n * t:(n + 1) * t, mp * HEAD_DIM:(mp + 1) * HEAD_DIM].astype(jnp.bfloat16)
            s = lax.dot_general(q_bf, kb, _NT, preferred_element_type=jnp.float32)
            s = s * scale + bias_ref[jnp.minimum(qi - n, N_BIAS_TILES - 1)]
            return jnp.where(jnp.where(n < qi, 1.0, causal) > 0.5, s, -jnp.inf)

        outs.append(_attend(nblk, lambda n: n <= qi, score_fn,
                            lambda n: v_ref[n * t:(n + 1) * t, :].astype(jnp.bfloat16),
                            s_ref, m_ref, l_ref, acc_ref))
    o = outs[0] - lam_ref[0] * outs[1]
    o = o * lax.rsqrt(jnp.mean(o * o, axis=-1, keepdims=True) + SUBLN_EPS) * g_ref[...] * out_scale
    o_ref[...] = o.astype(o_ref.dtype)


def diff_prompt(p3, bias_tiles, lam, subln_g, lam_init):
    b, t, _ = p3.shape
    nblk = t // ATT_TILE
    w = DIFF_VDIM
    return pl.pallas_call(
        functools.partial(_diff_prompt_kernel, nblk=nblk, out_scale=1.0 - lam_init),
        grid=(b, DIFF_HEADS, nblk),
        in_specs=[pl.BlockSpec(memory_space=pltpu.SMEM),
                  pl.BlockSpec((None, ATT_TILE, w), lambda bi, h, qi: (bi, qi, C_DQ // w + h)),
                  pl.BlockSpec((None, t, w), lambda bi, h, qi: (bi, 0, C_DK // w + h)),
                  pl.BlockSpec((None, t, w), lambda bi, h, qi: (bi, 0, C_DV // w + h)),
                  pl.BlockSpec((None, N_BIAS_TILES, ATT_TILE, ATT_TILE), lambda bi, h, qi: (h, 0, 0, 0)),
                  pl.BlockSpec((1, w), lambda bi, h, qi: (0, 0))],
        out_specs=pl.BlockSpec((None, ATT_TILE, w), lambda bi, h, qi: (bi, qi, h)),
        out_shape=jax.ShapeDtypeStruct((b, t, BRANCH_WIDTH), jnp.bfloat16),
        scratch_shapes=[pltpu.VMEM((ATT_TILE, t), jnp.float32),
                        pltpu.VMEM((ATT_TILE, 1), jnp.float32),
                        pltpu.VMEM((ATT_TILE, 1), jnp.float32),
                        pltpu.VMEM((ATT_TILE, w), jnp.float32)],
        compiler_params=pltpu.CompilerParams(
            dimension_semantics=("parallel", "parallel", "parallel"), vmem_limit_bytes=VMEM_LIMIT),
    )(lam.reshape(1).astype(jnp.float32), p3, p3, p3, bias_tiles, subln_g.reshape(1, w).astype(jnp.float32))


def _gelu_tanh(x):
    return 0.5 * x * (1.0 + jnp.tanh(math.sqrt(2.0 / math.pi) * (x + 0.044715 * (x * x * x))))


def _compress_prompt_kernel(x_ref, pe_ref, w1_ref, b1_ref, w2_ref, b2_ref, o_ref, *, n_ch):
    st = NSA_CMP_STRIDE
    pre_a = jnp.zeros((n_ch, NSA_CMP_HIDDEN), jnp.float32)
    pre_b = jnp.zeros((n_ch, NSA_CMP_HIDDEN), jnp.float32)
    for l in range(st):
        x_l = x_ref[pl.ds(l, n_ch, stride=st), :]
        pre_a += jnp.dot((x_l + pe_ref[l:l + 1, :]).astype(jnp.bfloat16), w1_ref[l],
                         preferred_element_type=jnp.float32)
        pre_b += jnp.dot((x_l + pe_ref[st + l:st + l + 1, :]).astype(jnp.bfloat16), w1_ref[st + l],
                         preferred_element_type=jnp.float32)
    pre = pre_a + pltpu.roll(pre_b, n_ch - 1, 0) + b1_ref[...]
    o_ref[...] = jnp.dot(_gelu_tanh(pre).astype(jnp.bfloat16), w2_ref[...],
                         preferred_element_type=jnp.float32) + b2_ref[...]


def compress_prompt(p3, cmp_pos, cmp_w1, cmp_b1, cmp_w2, cmp_b2):
    b, t, _ = p3.shape
    n_ch = t // NSA_CMP_STRIDE
    hd = HEAD_DIM
    g_n = NSA_GROUPS
    return pl.pallas_call(
        functools.partial(_compress_prompt_kernel, n_ch=n_ch),
        grid=(b, 2, g_n),
        in_specs=[pl.BlockSpec((None, t, hd), lambda bi, kv, g: (bi, 0, C_CK // hd + kv * g_n + g)),
                  pl.BlockSpec((None, NSA_CMP_BLOCK, hd), lambda bi, kv, g: (kv, 0, 0)),
                  pl.BlockSpec((None, NSA_CMP_BLOCK, hd, NSA_CMP_HIDDEN), lambda bi, kv, g: (kv, 0, 0, 0)),
                  pl.BlockSpec((None, 1, NSA_CMP_HIDDEN), lambda bi, kv, g: (kv, 0, 0)),
                  pl.BlockSpec((None, NSA_CMP_HIDDEN, hd), lambda bi, kv, g: (kv, 0, 0)),
                  pl.BlockSpec((None, 1, hd), lambda bi, kv, g: (kv, 0, 0))],
        out_specs=pl.BlockSpec((None, None, None, n_ch, hd), lambda bi, kv, g: (bi, kv, g, 0, 0)),
        out_shape=jax.ShapeDtypeStruct((b, 2, g_n, n_ch, hd), jnp.float32),
        compiler_params=pltpu.CompilerParams(
            dimension_semantics=("parallel", "parallel", "parallel"), vmem_limit_bytes=VMEM_LIMIT),
    )(p3, cmp_pos, cmp_w1.astype(jnp.bfloat16), cmp_b1.reshape(2, 1, NSA_CMP_HIDDEN),
      cmp_w2.astype(jnp.bfloat16), cmp_b2.reshape(2, 1, hd))


def _cmp_select_prompt_kernel(q_ref, kc_ref, vc_ref, o_ref, sel_ref, *, n_sel_blocks):
    t = ATT_TILE
    qi = pl.program_id(2)
    ncmp = kc_ref.shape[0]
    row = lax.broadcasted_iota(jnp.int32, (t, ncmp), 0)
    lane = lax.broadcasted_iota(jnp.int32, (t, ncmp), 1)
    q_pos = qi * t + row
    valid = (lane * NSA_CMP_STRIDE + NSA_CMP_BLOCK - 1 <= q_pos) & (lane < ncmp - 1)
    kc = kc_ref[...]
    vc = vc_ref[...].astype(jnp.bfloat16)
    scale = HEAD_DIM ** -0.5
    imp = jnp.zeros((t, ncmp), jnp.float32)
    for j in range(NSA_HPG):
        q = q_ref[:, j * HEAD_DIM:(j + 1) * HEAD_DIM]
        s = lax.dot_general(q, kc, _NT, precision=_HI, preferred_element_type=jnp.float32) * scale
        s = jnp.where(valid, s, -jnp.inf)
        m = jnp.max(s, axis=-1, keepdims=True)
        m = jnp.where(m == -jnp.inf, 0.0, m)
        e = jnp.exp(s - m)
        p = e / jnp.maximum(jnp.sum(e, axis=-1, keepdims=True), 1e-30)
        imp += p
        o_ref[:, j * HEAD_DIM:(j + 1) * HEAD_DIM] = jnp.dot(
            p.astype(jnp.bfloat16), vc, preferred_element_type=jnp.float32).astype(o_ref.dtype)
    ratio = NSA_SEL_BLOCK // NSA_CMP_STRIDE
    front = NSA_CMP_BLOCK // NSA_CMP_STRIDE - 1
    nn = lax.broadcasted_iota(jnp.int32, (ncmp, 128), 0)
    bb = lax.broadcasted_iota(jnp.int32, (ncmp, 128), 1)
    spread = ((nn >= ratio * bb - front) & (nn < ratio * bb + ratio) & (bb < n_sel_blocks)).astype(jnp.float32)
    slc = jnp.dot(imp, spread, precision=_HI, preferred_element_type=jnp.float32)
    row_b = lax.broadcasted_iota(jnp.int32, (t, 128), 0)
    blk = lax.broadcasted_iota(jnp.int32, (t, 128), 1)
    own = (qi * t + row_b) // NSA_SEL_BLOCK
    score = jnp.where((blk == 0) | (blk == own - 1), jnp.inf, slc)
    score = jnp.where(blk < own, score, -jnp.inf)
    sel = _select_top(score, min(NSA_SEL_COUNT - 1, n_sel_blocks - 1))
    sel_ref[...] = jnp.where(blk == own, 1.0, sel)


def cmp_select_prompt(p3, kv_cmp):
    b, t, _ = p3.shape
    nq = t // ATT_TILE
    gw = NSA_HPG * HEAD_DIM
    n_ch = kv_cmp.shape[3]
    return pl.pallas_call(
        functools.partial(_cmp_select_prompt_kernel, n_sel_blocks=t // NSA_SEL_BLOCK),
        grid=(b, NSA_GROUPS, nq),
        in_specs=[pl.BlockSpec((None, ATT_TILE, gw), lambda bi, g, qi: (bi, qi, C_NQ // gw + g)),
                  pl.BlockSpec((None, None, None, n_ch, HEAD_DIM), lambda bi, g, qi: (bi, 0, g, 0, 0)),
                  pl.BlockSpec((None, None, None, n_ch, HEAD_DIM), lambda bi, g, qi: (bi, 1, g, 0, 0))],
        out_specs=[pl.BlockSpec((None, ATT_TILE, gw), lambda bi, g, qi: (bi, qi, g)),
                   pl.BlockSpec((None, None, ATT_TILE, 128), lambda bi, g, qi: (bi, g, qi, 0))],
        out_shape=[jax.ShapeDtypeStruct((b, t, BRANCH_WIDTH), jnp.float32),
                   jax.ShapeDtypeStruct((b, NSA_GROUPS, t, 128), jnp.float32)],
        compiler_params=pltpu.CompilerParams(
            dimension_semantics=("parallel", "parallel", "parallel"), vmem_limit_bytes=VMEM_LIMIT),
    )(p3, kv_cmp, kv_cmp)


def _nsa_prompt_kernel(q_ref, ks_ref, vs_ref, kw_ref, vw_ref, sel_ref, ocmp_ref, ng_ref, bias_ref, o_ref,
                       keep_ref, s_ref, m_ref, l_ref, acc_ref, *, nblk):
    t = ATT_TILE
    g = pl.program_id(1)
    qi = pl.program_id(2)
    r, c = _tile_iotas()
    causal = _causal_tile()
    scale = HEAD_DIM ** -0.5
    per_tile = t // NSA_SEL_BLOCK
    sel = sel_ref[...]
    bb = lax.broadcasted_iota(jnp.int32, (128, t), 0)
    cc = lax.broadcasted_iota(jnp.int32, (128, t), 1)
    for n in range(nblk):
        @pl.when(n <= qi)
        def _(n=n):
            expand = (bb == per_tile * n + cc // NSA_SEL_BLOCK).astype(jnp.float32)
            keep_ref[:, n * t:(n + 1) * t] = jnp.dot(sel, expand, preferred_element_type=jnp.float32)
    gates = jax.nn.sigmoid(ng_ref[...])
    for j in range(NSA_HPG):
        q_bf = q_ref[:, j * HEAD_DIM:(j + 1) * HEAD_DIM].astype(jnp.bfloat16)

        def bias_of(n, j=j):
            return bias_ref[j, jnp.minimum(qi - n, N_BIAS_TILES - 1)]

        def sel_score(n, q_bf=q_bf, bias_of=bias_of):
            s = lax.dot_general(q_bf, ks_ref[n * t:(n + 1) * t, :].astype(jnp.bfloat16), _NT,
                                preferred_element_type=jnp.float32)
            s = s * scale + bias_of(n)
            keep = keep_ref[:, n * t:(n + 1) * t] * jnp.where(n < qi, 1.0, causal)
            return jnp.where(keep > 0.5, s, -jnp.inf)

        o_sel = _attend(nblk, lambda n: n <= qi, sel_score,
                        lambda n: vs_ref[n * t:(n + 1) * t, :].astype(jnp.bfloat16), s_ref, m_ref, l_ref, acc_ref)

        def win_score(n, q_bf=q_bf, bias_of=bias_of):
            s = lax.dot_general(q_bf, kw_ref[n * t:(n + 1) * t, :].astype(jnp.bfloat16), _NT,
                                preferred_element_type=jnp.float32)
            s = s * scale + bias_of(n)
            dist = (qi - n) * t + r - c
            return jnp.where((dist >= 0) & (dist < NSA_WINDOW), s, -jnp.inf)

        back = NSA_WINDOW // t
        o_win = _attend(nblk, lambda n: (n <= qi) & (n >= qi - back), win_score,
                        lambda n: vw_ref[n * t:(n + 1) * t, :].astype(jnp.bfloat16), s_ref, m_ref, l_ref, acc_ref)
        o_cmp = ocmp_ref[:, j * HEAD_DIM:(j + 1) * HEAD_DIM]
        gate_lane = lax.broadcasted_iota(jnp.int32, gates.shape, 1) - 3 * (g * NSA_HPG + j)
        g_cmp = jnp.sum(jnp.where(gate_lane == 0, gates, 0.0), axis=-1, keepdims=True)
        g_sel = jnp.sum(jnp.where(gate_lane == 1, gates, 0.0), axis=-1, keepdims=True)
        g_win = jnp.sum(jnp.where(gate_lane == 2, gates, 0.0), axis=-1, keepdims=True)
        o = g_cmp * o_cmp + g_sel * o_sel + g_win * o_win
        o_ref[:, j * HEAD_DIM:(j + 1) * HEAD_DIM] = o.astype(o_ref.dtype)


def nsa_prompt(p3, sel_mask, o_cmp, bias_tiles):
    b, t, _ = p3.shape
    nblk = t // ATT_TILE
    hd = HEAD_DIM
    gw = NSA_HPG * hd
    g_n = NSA_GROUPS

    def kv_spec(col):
        return pl.BlockSpec((None, t, hd), lambda bi, g, qi: (bi, 0, col // hd + g))

    return pl.pallas_call(
        functools.partial(_nsa_prompt_kernel, nblk=nblk),
        grid=(b, g_n, nblk),
        in_specs=[pl.BlockSpec((None, ATT_TILE, gw), lambda bi, g, qi: (bi, qi, C_NQ // gw + g)),
                  kv_spec(C_SK), kv_spec(C_SV), kv_spec(C_WK), kv_spec(C_WV),
                  pl.BlockSpec((None, None, ATT_TILE, 128), lambda bi, g, qi: (bi, g, qi, 0)),
                  pl.BlockSpec((None, ATT_TILE, gw), lambda bi, g, qi: (bi, qi, g)),
                  pl.BlockSpec((None, ATT_TILE, 128), lambda bi, g, qi: (bi, qi, C_NG // 128)),
                  pl.BlockSpec((None, NSA_HPG, N_BIAS_TILES, ATT_TILE, ATT_TILE),
                               lambda bi, g, qi: (g, 0, 0, 0, 0))],
        out_specs=pl.BlockSpec((None, ATT_TILE, gw), lambda bi, g, qi: (bi, qi, g)),
        out_shape=jax.ShapeDtypeStruct((b, t, BRANCH_WIDTH), jnp.bfloat16),
        scratch_shapes=[pltpu.VMEM((ATT_TILE, t), jnp.float32),
                        pltpu.VMEM((ATT_TILE, t), jnp.float32),
                        pltpu.VMEM((ATT_TILE, 1), jnp.float32),
                        pltpu.VMEM((ATT_TILE, 1), jnp.float32),
                        pltpu.VMEM((ATT_TILE, hd), jnp.float32)],
        compiler_params=pltpu.CompilerParams(
            dimension_semantics=("parallel", "parallel", "parallel"), vmem_limit_bytes=VMEM_LIMIT),
    )(p3, p3, p3, p3, p3, sel_mask, o_cmp, p3, bias_tiles)


def _page_softmax(s):
    m = jnp.max(s, axis=-1, keepdims=True)
    e = jnp.exp(s - jnp.where(m == -jnp.inf, 0.0, m))
    return m, e, jnp.sum(e, axis=-1, keepdims=True)


def _is_last(p, last_page):
    return (p == last_page).astype(jnp.int32)


PAGES_PER_STEP = 8


def _moba_pages_kernel(pt_ref, q_ref, *refs, last_page):
    kv_refs = refs[:PAGES_PER_STEP]
    bias_ref, o_ref, m_ref, l_ref, ks_ref = refs[PAGES_PER_STEP:]
    q = q_ref[...]
    for k, kv_ref in enumerate(kv_refs):
        p = pl.program_id(1) * PAGES_PER_STEP + k
        k3 = kv_ref[:, 0]
        v3 = kv_ref[:, 1]
        rows = k3.shape[0] * k3.shape[1]
        ks_ref[k] = jnp.sum(k3, axis=0)
        kx = k3.reshape(rows, HEAD_DIM).astype(jnp.bfloat16)
        vx = v3.reshape(rows, HEAD_DIM).astype(jnp.bfloat16)
        s = lax.dot_general(q, kx, _NT, preferred_element_type=jnp.float32)
        s = s * HEAD_DIM ** -0.5 + bias_ref[_is_last(p, last_page)]
        row = lax.broadcasted_iota(jnp.int32, s.shape, 0)
        lane = lax.broadcasted_iota(jnp.int32, s.shape, 1)
        s = jnp.where((lane & (MOBA_HEADS - 1)) == (row & (MOBA_HEADS - 1)), s, -jnp.inf)
        m, e, l = _page_softmax(s)
        o_ref[k] = jnp.dot(e.astype(jnp.bfloat16), vx, preferred_element_type=jnp.float32)
        m_ref[k] = m
        l_ref[k] = l


def _page_specs(block, index_of_page):
    return [pl.BlockSpec(block, functools.partial(
        lambda bi, i, pt, k: index_of_page(pt[bi, i * PAGES_PER_STEP + k]), k=k)) for k in range(PAGES_PER_STEP)]


def _page_stat_specs(n_rows, width):
    def im(bi, i, pt):
        return (bi, i, 0, 0)
    return [pl.BlockSpec((None, PAGES_PER_STEP, n_rows, width), im),
            pl.BlockSpec((None, PAGES_PER_STEP, n_rows, 1), im),
            pl.BlockSpec((None, PAGES_PER_STEP, n_rows, 1), im)]


def _page_stat_shapes(b, pages, n_rows, width):
    return [jax.ShapeDtypeStruct((b, pages, n_rows, width), jnp.float32),
            jax.ShapeDtypeStruct((b, pages, n_rows, 1), jnp.float32),
            jax.ShapeDtypeStruct((b, pages, n_rows, 1), jnp.float32)]


_PAGE_PARAMS = pltpu.CompilerParams(dimension_semantics=("parallel", "parallel"), vmem_limit_bytes=VMEM_LIMIT)


def moba_pages(q_rows, cache, layer, page_table, bias2):
    b, n_rows, _ = q_rows.shape
    pages = page_table.shape[1]
    page = cache.shape[2]
    assert MOBA_HEADS & (MOBA_HEADS - 1) == 0 and pages % PAGES_PER_STEP == 0
    grid_spec = pltpu.PrefetchScalarGridSpec(
        num_scalar_prefetch=1, grid=(b, pages // PAGES_PER_STEP),
        in_specs=[pl.BlockSpec((None, n_rows, HEAD_DIM), lambda bi, i, pt: (bi, 0, 0))]
        + _page_specs((None, None, page, 2, MOBA_HEADS, HEAD_DIM), lambda phys: (layer, phys, 0, 0, 0, 0))
        + [pl.BlockSpec((2, n_rows, page * MOBA_HEADS), lambda bi, i, pt: (0, 0, 0))],
        out_specs=_page_stat_specs(n_rows, HEAD_DIM) + [
            pl.BlockSpec((None, PAGES_PER_STEP, MOBA_HEADS, HEAD_DIM), lambda bi, i, pt: (bi, i, 0, 0))])
    return pl.pallas_call(
        functools.partial(_moba_pages_kernel, last_page=pages - 1),
        grid_spec=grid_spec,
        out_shape=_page_stat_shapes(b, pages, n_rows, HEAD_DIM) + [
            jax.ShapeDtypeStruct((b, pages, MOBA_HEADS, HEAD_DIM), jnp.float32)],
        compiler_params=_PAGE_PARAMS,
    )(page_table, q_rows, *([cache] * PAGES_PER_STEP), bias2)


def _diff_pages_kernel(pt_ref, q_ref, *refs, last_page):
    kv_refs = refs[:PAGES_PER_STEP]
    bias_ref, o_ref, m_ref, l_ref = refs[PAGES_PER_STEP:]
    q = q_ref[...]
    for k, kv_ref in enumerate(kv_refs):
        p = pl.program_id(1) * PAGES_PER_STEP + k
        half = kv_ref.shape[1] // 2
        s = lax.dot_general(q, kv_ref[:, :half], _NT, preferred_element_type=jnp.float32)
        s = s * HEAD_DIM ** -0.5 + bias_ref[_is_last(p, last_page)]
        m, e, l = _page_softmax(s)
        pv = jnp.dot(e.astype(jnp.bfloat16), kv_ref[:, half:], preferred_element_type=jnp.float32)
        rph = s.shape[0] // DIFF_HEADS
        for h in range(DIFF_HEADS):
            o_ref[k, h * rph:(h + 1) * rph, :] = pv[h * rph:(h + 1) * rph, h * DIFF_VDIM:(h + 1) * DIFF_VDIM]
        m_ref[k] = m
        l_ref[k] = l


def diff_pages(q_blk, cache2d, layer, page_table, bias2):
    b, n_rows, kw = q_blk.shape
    pages = page_table.shape[1]
    page = cache2d.shape[2]
    assert pages % PAGES_PER_STEP == 0
    grid_spec = pltpu.PrefetchScalarGridSpec(
        num_scalar_prefetch=1, grid=(b, pages // PAGES_PER_STEP),
        in_specs=[pl.BlockSpec((None, n_rows, kw), lambda bi, i, pt: (bi, 0, 0))]
        + _page_specs((None, None, page, 2 * kw), lambda phys: (layer, phys, 0, 0))
        + [pl.BlockSpec((2, n_rows, page), lambda bi, i, pt: (0, 0, 0))],
        out_specs=_page_stat_specs(n_rows, DIFF_VDIM))
    return pl.pallas_call(
        functools.partial(_diff_pages_kernel, last_page=pages - 1),
        grid_spec=grid_spec,
        out_shape=_page_stat_shapes(b, pages, n_rows, DIFF_VDIM),
        compiler_params=_PAGE_PARAMS,
    )(page_table, q_blk, *([cache2d] * PAGES_PER_STEP), bias2)


NSA_RECORD = 4 * NSA_GROUPS * HEAD_DIM


def _compress_pages_kernel(pt_ref, pg0_ref, pg1_ref, pe_ref, w1_ref, b1_ref, w2_ref, b2_ref, o_ref, cbuf_ref,
                           *, n_steps):
    i = pl.program_id(1)
    st = NSA_CMP_STRIDE
    page = pg0_ref.shape[0]
    chunks = page // st
    base = pl.multiple_of(i * 2 * chunks, 2 * chunks)
    r = lax.broadcasted_iota(jnp.int32, (page, page), 0)
    tok = lax.broadcasted_iota(jnp.int32, (page, page), 1)
    assert chunks & (chunks - 1) == 0
    perm = (tok == (r & (chunks - 1)) * st + (r >> (chunks.bit_length() - 1))).astype(jnp.bfloat16)
    for kg in range(2 * NSA_GROUPS):
        lanes = slice(kg * HEAD_DIM, (kg + 1) * HEAD_DIM)
        y0 = jnp.dot(perm, pg0_ref[:, lanes].astype(jnp.bfloat16), preferred_element_type=jnp.float32)
        y1 = jnp.dot(perm, pg1_ref[:, lanes].astype(jnp.bfloat16), preferred_element_type=jnp.float32)
        for l in range(st):
            x0 = y0[l * chunks:(l + 1) * chunks]
            x1 = y1[l * chunks:(l + 1) * chunks]
            cbuf_ref[kg, pl.ds(base, 2 * chunks), l * HEAD_DIM:(l + 1) * HEAD_DIM] = (
                jnp.concatenate([x0, x1], axis=0).astype(jnp.bfloat16))

    @pl.when(i == n_steps - 1)
    def _():
        n_ch = cbuf_ref.shape[1]
        half = st * HEAD_DIM
        for kind in range(2):
            const = jnp.dot(pe_ref[kind].astype(jnp.bfloat16), w1_ref[kind],
                            preferred_element_type=jnp.float32)[0:1] + b1_ref[kind]
            for g in range(NSA_GROUPS):
                c = cbuf_ref[kind * NSA_GROUPS + g]
                pre_a = jnp.dot(c, w1_ref[kind, :half, :], preferred_element_type=jnp.float32)
                pre_b = jnp.dot(c, w1_ref[kind, half:, :], preferred_element_type=jnp.float32)
                pre = pre_a + pltpu.roll(pre_b, n_ch - 1, 0) + const
                o_ref[kind, g] = jnp.dot(_gelu_tanh(pre).astype(jnp.bfloat16), w2_ref[kind],
                                         preferred_element_type=jnp.float32) + b2_ref[kind]


def compress_pages(cache2d, layer, page_table, cmp_pos, cmp_w1, cmp_b1, cmp_w2, cmp_b2):
    b, pages = page_table.shape
    page = cache2d.shape[2]
    n_ch = pages * page // NSA_CMP_STRIDE
    feat = NSA_CMP_BLOCK * HEAD_DIM
    assert pages % 2 == 0 and (2 * page // NSA_CMP_STRIDE) % 16 == 0 and cache2d.shape[3] == NSA_RECORD
    pe = jnp.broadcast_to(cmp_pos.reshape(2, 1, feat), (2, 8, feat))
    grid_spec = pltpu.PrefetchScalarGridSpec(
        num_scalar_prefetch=1, grid=(b, pages // 2),
        in_specs=[pl.BlockSpec((None, None, page, NSA_RECORD), lambda bi, i, pt: (layer, pt[bi, 2 * i], 0, 0)),
                  pl.BlockSpec((None, None, page, NSA_RECORD), lambda bi, i, pt: (layer, pt[bi, 2 * i + 1], 0, 0)),
                  pl.BlockSpec((2, 8, feat), lambda bi, i, pt: (0, 0, 0)),
                  pl.BlockSpec((2, feat, NSA_CMP_HIDDEN), lambda bi, i, pt: (0, 0, 0)),
                  pl.BlockSpec((2, 1, NSA_CMP_HIDDEN), lambda bi, i, pt: (0, 0, 0)),
                  pl.BlockSpec((2, NSA_CMP_HIDDEN, HEAD_DIM), lambda bi, i, pt: (0, 0, 0)),
                  pl.BlockSpec((2, 1, HEAD_DIM), lambda bi, i, pt: (0, 0, 0))],
        out_specs=pl.BlockSpec((None, 2, NSA_GROUPS, n_ch, HEAD_DIM), lambda bi, i, pt: (bi, 0, 0, 0, 0)),
        scratch_shapes=[pltpu.VMEM((2 * NSA_GROUPS, n_ch, NSA_CMP_STRIDE * HEAD_DIM), jnp.bfloat16)])
    return pl.pallas_call(
        functools.partial(_compress_pages_kernel, n_steps=pages // 2),
        grid_spec=grid_spec,
        out_shape=jax.ShapeDtypeStruct((b, 2, NSA_GROUPS, n_ch, HEAD_DIM), jnp.float32),
        compiler_params=pltpu.CompilerParams(
            dimension_semantics=("parallel", "arbitrary"), vmem_limit_bytes=VMEM_LIMIT),
    )(page_table, cache2d, cache2d, pe, cmp_w1.reshape(2, feat, NSA_CMP_HIDDEN).astype(jnp.bfloat16),
      cmp_b1.reshape(2, 1, NSA_CMP_HIDDEN), cmp_w2.astype(jnp.bfloat16), cmp_b2.reshape(2, 1, HEAD_DIM))


def _nsa_sel_pages_kernel(pt_ref, q_ref, *refs, last_page):
    pg_refs = refs[:PAGES_PER_STEP]
    km_ref, bias_ref, o_ref, m_ref, l_ref = refs[PAGES_PER_STEP:]
    tokens = pg_refs[0].shape[0]
    rpg = q_ref.shape[0] // NSA_GROUPS
    for k, pg_ref in enumerate(pg_refs):
        p = pl.program_id(1) * PAGES_PER_STEP + k
        bias = bias_ref[_is_last(p, last_page)]
        for g in range(NSA_GROUPS):
            rows = slice(g * rpg, (g + 1) * rpg)
            k_lane = (2 * NSA_GROUPS + g) * HEAD_DIM
            v_lane = (3 * NSA_GROUPS + g) * HEAD_DIM
            kx = pg_ref[:, k_lane:k_lane + HEAD_DIM].astype(jnp.bfloat16)
            vx = pg_ref[:, v_lane:v_lane + HEAD_DIM].astype(jnp.bfloat16)
            s = lax.dot_general(q_ref[rows, :], kx, _NT, preferred_element_type=jnp.float32)
            s = s * HEAD_DIM ** -0.5 + bias[rows]
            s = jnp.where(km_ref[rows, k * tokens:(k + 1) * tokens] > 0.5, s, -jnp.inf)
            m, e, l = _page_softmax(s)
            o_ref[k, rows, :] = jnp.dot(e.astype(jnp.bfloat16), vx, preferred_element_type=jnp.float32)
            m_ref[k, rows, :] = m
            l_ref[k, rows, :] = l


def nsa_sel_pages(q_rows, cache2d, layer, page_table, key_mask, bias2):
    b, n_rows, _ = q_rows.shape
    pages = page_table.shape[1]
    page = cache2d.shape[2]
    assert cache2d.shape[3] == NSA_RECORD and pages % PAGES_PER_STEP == 0 and (n_rows // NSA_GROUPS) % 8 == 0
    grid_spec = pltpu.PrefetchScalarGridSpec(
        num_scalar_prefetch=1, grid=(b, pages // PAGES_PER_STEP),
        in_specs=[pl.BlockSpec((None, n_rows, HEAD_DIM), lambda bi, i, pt: (bi, 0, 0))]
        + _page_specs((None, None, page, NSA_RECORD), lambda phys: (layer, phys, 0, 0))
        + [pl.BlockSpec((None, n_rows, PAGES_PER_STEP * page), lambda bi, i, pt: (bi, 0, i)),
           pl.BlockSpec((2, n_rows, page), lambda bi, i, pt: (0, 0, 0))],
        out_specs=_page_stat_specs(n_rows, HEAD_DIM))
    return pl.pallas_call(
        functools.partial(_nsa_sel_pages_kernel, last_page=pages - 1),
        grid_spec=grid_spec,
        out_shape=_page_stat_shapes(b, pages, n_rows, HEAD_DIM),
        compiler_params=_PAGE_PARAMS,
    )(page_table, q_rows, *([cache2d] * PAGES_PER_STEP), key_mask, bias2)


def _merge_pages(m_pg, l_pg, o_pg, w_pg, s_own, v_own):
    m_sel = jnp.where(w_pg, m_pg, -jnp.inf)
    top = jnp.maximum(jnp.max(m_sel, axis=-1), jnp.max(s_own, axis=-1))[..., None]
    wp = jnp.exp(m_sel - top)
    e_own = jnp.exp(s_own - top)
    den = jnp.sum(wp * l_pg, axis=-1) + jnp.sum(e_own, axis=-1)
    num = jnp.sum(wp[..., None] * o_pg, axis=-2) + jnp.sum(e_own[..., None] * v_own, axis=-2)
    return num / den[..., None]


def _decode_bias(strip, t, page):
    assert page >= REL_MAX_DIST
    n_ab = MOBA_HEADS + DIFF_HEADS
    d_last = jnp.clip(page + jnp.arange(t)[:, None] - jnp.arange(page)[None, :], 0, REL_MAX_DIST)
    last = strip[d_last]
    far = jnp.broadcast_to(strip[REL_MAX_DIST], last.shape)
    out = []
    for src in (far, last):
        a = jnp.transpose(src[:, :, :MOBA_HEADS], (0, 2, 1)).reshape(t * MOBA_HEADS, page)
        a = jnp.repeat(a, MOBA_HEADS, axis=1)
        bb = jnp.transpose(src[:, :, MOBA_HEADS:n_ab], (2, 0, 1))
        bb = jnp.broadcast_to(bb[:, None], (DIFF_HEADS, 2, t, page)).reshape(DIFF_HEADS * 2 * t, page)
        c = src[:, :, n_ab:].reshape(t, page, NSA_GROUPS, NSA_HPG)
        c = jnp.transpose(c, (2, 0, 3, 1)).reshape(NSA_GROUPS * t * NSA_HPG, page)
        out.append((a, bb, c))
    return tuple(jnp.stack([out[0][i], out[1][i]]) for i in range(3))


def _masked_softmax(s, mask, axis):
    s = jnp.where(mask, s, -jnp.inf)
    m = jnp.max(s, axis=axis, keepdims=True)
    m = jnp.where(jnp.isfinite(m), m, 0.0)
    e = jnp.where(mask, jnp.exp(s - m), 0.0)
    return e / jnp.maximum(jnp.sum(e, axis=axis, keepdims=True), 1e-30)


def _topk_mask(score, n_top):
    nb = score.shape[-1]
    iota = lax.broadcasted_iota(jnp.int32, score.shape, score.ndim - 1)
    sel = jnp.zeros(score.shape, bool)
    for _ in range(n_top):
        m = jnp.max(score, axis=-1, keepdims=True)
        first = jnp.min(jnp.where(score == m, iota, nb), axis=-1, keepdims=True)
        hit = (iota == first) & (m > -jnp.inf)
        sel = sel | hit
        score = jnp.where(iota == first, -jnp.inf, score)
    return sel


def _bias_of_dist(strip, q_pos, k_pos):
    d = jnp.clip(q_pos[:, None] - k_pos[None, :], 0, REL_MAX_DIST)
    return strip[d]


def _own_logits(s, strip_h, t):
    tq = jnp.arange(t)
    dist = tq[:, None] - tq[None, :]
    s = s * HEAD_DIM ** -0.5 + strip_h[jnp.clip(dist, 0, REL_MAX_DIST)]
    return jnp.where((dist >= 0)[..., None], s, -jnp.inf)


def _decode_moba(p3, q_start, layer, cache, page_table, bias2, strip_a):
    b, t, _ = p3.shape
    h, d = MOBA_HEADS, HEAD_DIM
    pages = page_table.shape[1]
    page = cache.shape[2]
    ppb = MOBA_BLOCK // page
    assert q_start == pages * page and q_start % MOBA_BLOCK == 0 and t <= MOBA_BLOCK and MOBA_BLOCK % page == 0
    q = p3[:, :, C_MQ:C_MQ + _BW].reshape(b, t, h, d)
    k_new = p3[:, :, C_MK:C_MK + _BW].reshape(b, t, h, d)
    v_new = p3[:, :, C_MV:C_MV + _BW].reshape(b, t, h, d)
    o_pg, m_pg, l_pg, ksum = moba_pages(q.reshape(b, t * h, d).astype(jnp.bfloat16), cache, layer, page_table, bias2)
    k_mean = jnp.sum(ksum.reshape(b, pages // ppb, ppb, h, d), axis=2) / MOBA_BLOCK
    score = jnp.einsum('bthd,bnhd->bthn', q, k_mean, precision=lax.Precision.HIGHEST)
    sel = _topk_mask(score, min(MOBA_TOPK, q_start // MOBA_BLOCK))
    w_pg = jnp.repeat(sel, ppb, axis=-1)

    def per_query(a):
        return jnp.transpose(a.reshape(b, pages, t, h, a.shape[-1]), (0, 2, 3, 1, 4))

    s_own = _own_logits(jnp.einsum('bthd,bshd->btsh', q, k_new), strip_a, t)
    s_own = jnp.transpose(s_own, (0, 1, 3, 2))
    v_own = jnp.transpose(v_new, (0, 2, 1, 3))[:, None]
    o = _merge_pages(per_query(m_pg)[..., 0], per_query(l_pg)[..., 0], per_query(o_pg), w_pg, s_own, v_own)
    return o.reshape(b, t, _BW)


def _decode_diff(p3, q_start, layer, cache2d, page_table, bias2, strip_b, lam, subln_g, lam_init):
    b, t, _ = p3.shape
    h, d = DIFF_HEADS, HEAD_DIM
    pages = page_table.shape[1]
    assert q_start == pages * cache2d.shape[2]
    q = p3[:, :, C_DQ:C_DQ + _BW].reshape(b, t, h, 2, d)
    k_new = p3[:, :, C_DK:C_DK + _BW].reshape(b, t, h, 2, d)
    v_new = p3[:, :, C_DV:C_DV + _BW].reshape(b, t, h, DIFF_VDIM)
    q_rows = jnp.transpose(q, (0, 2, 3, 1, 4)).reshape(b, 2 * h, t, 1, d)
    q_blk = (q_rows * jnp.eye(2 * h, dtype=q.dtype)[None, :, None, :, None]).reshape(b, 2 * h * t, 2 * h * d)
    o_pg, m_pg, l_pg = diff_pages(q_blk.astype(jnp.bfloat16), cache2d, layer, page_table, bias2)

    def per_query(a):
        return jnp.transpose(a.reshape(b, pages, h, 2, t, a.shape[-1]), (0, 2, 3, 4, 1, 5))

    s_own = _own_logits(jnp.einsum('bthmd,bshmd->bmtsh', q, k_new), strip_b, t)
    s_own = jnp.transpose(s_own, (0, 4, 1, 2, 3))
    v_own = jnp.transpose(v_new, (0, 2, 1, 3))[:, :, None, None]
    w_pg = jnp.ones((b, h, 2, t, pages), bool)
    o = _merge_pages(per_query(m_pg)[..., 0], per_query(l_pg)[..., 0], per_query(o_pg), w_pg, s_own, v_own)
    o = jnp.transpose(o[:, :, 0] - lam * o[:, :, 1], (0, 2, 1, 3))
    o = o * lax.rsqrt(jnp.mean(o * o, axis=-1, keepdims=True) + SUBLN_EPS) * subln_g * (1.0 - lam_init)
    return o.reshape(b, t, _BW)


def _cmp_attention_select(q, q_start, k_cmp, v_cmp, l):
    b, t, g, j, d = q.shape
    n_cmp = k_cmp.shape[2]
    q_pos = q_start + jnp.arange(t)
    cmp_end = jnp.arange(n_cmp) * NSA_CMP_STRIDE + NSA_CMP_BLOCK - 1
    valid = cmp_end[None, :] <= q_pos[:, None]
    s = jnp.einsum('btgjd,bgnd->bgjtn', q, k_cmp, precision=lax.Precision.HIGHEST) * d ** -0.5
    p = _masked_softmax(s, valid, -1)
    o_cmp = jnp.einsum('bgjtn,bgnd->btgjd', p, v_cmp)
    imp = jnp.sum(p, axis=2)
    nbs = -(-l // NSA_SEL_BLOCK)
    ratio = NSA_SEL_BLOCK // NSA_CMP_STRIDE
    front = NSA_CMP_BLOCK // NSA_CMP_STRIDE - 1
    span = ratio + front
    back = max(ratio * (nbs - 1) + span - front - n_cmp, 0)
    imp = jnp.pad(imp, ((0, 0), (0, 0), (0, 0), (front, back)))
    slc = sum(imp[..., o:o + ratio * nbs:ratio] for o in range(span))
    slc = jnp.transpose(slc, (0, 2, 1, 3))
    own = q_pos // NSA_SEL_BLOCK
    blk = jnp.arange(nbs)
    elig = blk[None, :] < own[:, None]
    forced = (blk[None, :] == 0) | (blk[None, :] == own[:, None] - 1)
    score = jnp.where(forced[None, :, None, :], jnp.inf, slc)
    score = jnp.where(elig[None, :, None, :], score, -jnp.inf)
    n_top = min(NSA_SEL_COUNT - 1, (q_start + t - 1) // NSA_SEL_BLOCK)
    sel = _topk_mask(score, n_top) | (blk[None, :] == own[:, None])[None, :, None, :]
    return o_cmp, sel


def _decode_nsa(p3, q_start, layer, cache2d, page_table, bias2, strip_c, win_buf, cmp_w):
    b, t, _ = p3.shape
    g, j, d = NSA_GROUPS, NSA_HPG, HEAD_DIM
    pages = page_table.shape[1]
    page = cache2d.shape[2]
    past = pages * page
    assert q_start == past and past % NSA_SEL_BLOCK == 0 and t <= NSA_SEL_BLOCK and (q_start + t) // NSA_CMP_STRIDE == past // NSA_CMP_STRIDE
    q = p3[:, :, C_NQ:C_NQ + _BW].reshape(b, t, g, j, d)
    kv_cmp = compress_pages(cache2d, layer, page_table, *cmp_w)
    n_cmp = past // NSA_CMP_STRIDE - NSA_CMP_BLOCK // NSA_CMP_STRIDE + 1
    o_cmp, sel = _cmp_attention_select(q, q_start, kv_cmp[:, 0, :, :n_cmp], kv_cmp[:, 1, :, :n_cmp], q_start + t)
    key_mask = jnp.repeat(sel[..., :past // NSA_SEL_BLOCK], NSA_SEL_BLOCK, axis=-1)
    key_mask = jnp.broadcast_to(jnp.transpose(key_mask, (0, 2, 1, 3))[:, :, :, None], (b, g, t, j, past))
    q_rows = jnp.transpose(q, (0, 2, 1, 3, 4)).reshape(b, g * t * j, d)
    o_pg, m_pg, l_pg = nsa_sel_pages(q_rows.astype(jnp.bfloat16), cache2d, layer, page_table,
                                     key_mask.reshape(b, g * t * j, past).astype(jnp.float32), bias2)

    def per_query(a):
        return jnp.transpose(a.reshape(b, pages, g, t, j, a.shape[-1]), (0, 2, 3, 4, 1, 5))

    sk_new = p3[:, :, C_SK:C_SK + _GW].reshape(b, t, g, d)
    sv_new = p3[:, :, C_SV:C_SV + _GW].reshape(b, t, g, d)
    s_own = _own_logits(jnp.einsum('btgjd,bsgd->btsgj', q, sk_new).reshape(b, t, t, g * j), strip_c, t)
    s_own = jnp.transpose(s_own.reshape(b, t, t, g, j), (0, 3, 1, 4, 2))
    v_own = jnp.transpose(sv_new, (0, 2, 1, 3))[:, :, None, None]
    w_pg = jnp.ones((b, g, t, j, pages), bool)
    o_sel = _merge_pages(per_query(m_pg)[..., 0], per_query(l_pg)[..., 0], per_query(o_pg), w_pg, s_own, v_own)
    o_sel = jnp.transpose(o_sel, (0, 2, 1, 3, 4))

    win_rows = p3[:, :, C_WK:C_WK + 2 * _GW].reshape(b, t, 2, g, d)
    w = win_buf.shape[1]
    kv = jnp.concatenate([win_buf, win_rows], axis=1)
    o_win = window_attention(q, q_start + jnp.arange(t), kv[:, :, 0], kv[:, :, 1],
                             q_start - w + jnp.arange(w + t), strip_c)
    g_nsa = jax.nn.sigmoid(p3[:, :, C_NG:C_NG + N_NG].reshape(b, t, g, j, 3))
    o_c = g_nsa[..., 0:1] * o_cmp + g_nsa[..., 1:2] * o_sel + g_nsa[..., 2:3] * o_win
    return o_c.reshape(b, t, _BW), kv[:, t:]


def window_attention(q, q_pos, k, v, k_pos, strip):
    b, t, g, j, d = q.shape
    l = k.shape[1]
    dist = q_pos[:, None] - k_pos[None, :]
    mask = (k_pos[None, :] >= 0) & (dist >= 0) & (dist < NSA_WINDOW)
    bias = jnp.transpose(_bias_of_dist(strip, q_pos, k_pos).reshape(t, l, g, j), (2, 3, 0, 1))
    s = jnp.einsum('btgjd,bsgd->bgjts', q, k) * d ** -0.5 + bias[None]
    p = _masked_softmax(s, mask, -1)
    return jnp.einsum('bgjts,bsgd->btgjd', p, v)


def _layer(x, q_start, past, win_buf, conv_buf, bias, lam_init, wts):
    (g_attn, w_in_p, lam_p, subln_g, cmp_pos, cmp_w1, cmp_b1, cmp_w2, cmp_b2,
     w_branch, w_out, g_ffn, w_gate, w_up, w_down, conv_w, conv_b) = wts
    b, t, _ = x.shape
    m = b * t
    hd = HEAD_DIM
    x2 = x.reshape(m, D_MODEL)
    h = rmsnorm_rows(x2, g_attn, RMS_EPS, jnp.bfloat16)
    proj = matmul(h, w_in_p)
    p3 = proj.reshape(b, t, PROJ_PACKED)

    def cols(start, width):
        return p3[:, :, start:start + width]

    moba_rows = cols(C_MK, 2 * _BW).reshape(b, t, 2, MOBA_HEADS, hd)
    diff_rows = cols(C_DK, 2 * _BW).reshape(b, t, 2, DIFF_HEADS, DIFF_VDIM)
    nsa_rows = cols(C_CK, 4 * _GW).reshape(b, t, 4, NSA_GROUPS, hd)
    win_rows = cols(C_WK, 2 * _GW).reshape(b, t, 2, NSA_GROUPS, hd)
    strip, tiles = bias
    n_ab = MOBA_HEADS + DIFF_HEADS
    lam_f = lam_p.astype(jnp.float32)
    lam = jnp.exp(jnp.sum(lam_f[0] * lam_f[1])) - jnp.exp(jnp.sum(lam_f[2] * lam_f[3])) + lam_init
    if past is None:
        o_a = moba_prompt(p3, tiles[:MOBA_HEADS])
        o_b = diff_prompt(p3, tiles[MOBA_HEADS:n_ab], lam, subln_g, lam_init)
        kv_cmp = compress_prompt(p3, cmp_pos, cmp_w1, cmp_b1, cmp_w2, cmp_b2)
        o_cmp, sel_mask = cmp_select_prompt(p3, kv_cmp)
        o_c = nsa_prompt(p3, sel_mask, o_cmp,
                         tiles[n_ab:].reshape(NSA_GROUPS, NSA_HPG, N_BIAS_TILES, ATT_TILE, ATT_TILE))
        new_win = win_rows[:, t - min(NSA_WINDOW, t):]
    else:
        layer, (cache_a, cache_b, cache_c), page_table, (bias_a, bias_b, bias_c) = past
        o_a = _decode_moba(p3, q_start, layer, cache_a, page_table, bias_a, strip[:, :MOBA_HEADS])
        o_b = _decode_diff(p3, q_start, layer, cache_b, page_table, bias_b, strip[:, MOBA_HEADS:n_ab],
                           lam, subln_g, lam_init)
        o_c, new_win = _decode_nsa(p3, q_start, layer, cache_c, page_table, bias_c, strip[:, n_ab:], win_buf,
                                   (cmp_pos, cmp_w1, cmp_b1, cmp_w2, cmp_b2))
    branches = [o.reshape(m, _BW).astype(jnp.bfloat16) for o in (o_a, o_b, o_c)]
    mix = merge_branches(branches, w_branch, proj)
    x1 = matmul(mix, w_out, x2)

    hf = rmsnorm_rows(x1, g_ffn, RMS_EPS, jnp.bfloat16)
    if conv_buf is None:
        tm = 512
        hprev = hf.reshape(b, t // tm, tm, D_MODEL)[:, :, tm - 8:, :]
        hprev = jnp.concatenate([jnp.zeros_like(hprev[:, :1]), hprev[:, :-1]], axis=1)
        gprev = matmul(hprev.reshape(m // tm * 8, D_MODEL), w_gate).reshape(m // tm, 8, D_FF_PAD)
        act = ffn_in(hf, gprev, w_gate, w_up, conv_w, conv_b, tm=tm)
        g_last = matmul(hf.reshape(b, t, D_MODEL)[:, t - 8:].reshape(b * 8, D_MODEL), w_gate)
        new_conv = g_last.reshape(b, 8, D_FF_PAD)[:, 8 - (CONV_WIDTH - 1):, :D_FF]
    else:
        g = matmul(hf, w_gate).reshape(b, t, D_FF_PAD)
        u = matmul(hf, w_up)
        prev = jnp.pad(conv_buf, ((0, 0), (0, 0), (0, D_FF_PAD - D_FF)))
        gp = jnp.concatenate([prev, g], axis=1)
        shifted = [gp[:, i:i + t].reshape(m, D_FF_PAD) for i in range(CONV_WIDTH)]
        act = conv_act(*shifted, u, conv_w, conv_b)
        new_conv = gp[:, t:, :D_FF]
    x2o = matmul(act, w_down, x1, tm=1024, tn=1024, tk=2816)
    return x2o.reshape(b, t, D_MODEL), moba_rows, diff_rows, nsa_rows, new_win, new_conv


def _page_views(cache_moba_kv, cache_diff_kv, cache_nsa_kv):
    dd, n_pool, page = cache_diff_kv.shape[:3]
    diff2d = cache_diff_kv.reshape(dd, n_pool, page, -1).astype(jnp.bfloat16)
    nsa2d = cache_nsa_kv.reshape(cache_nsa_kv.shape[0], cache_nsa_kv.shape[1], page, -1)
    return cache_moba_kv, diff2d, nsa2d


def _pack_w_in(w_in):
    head = w_in[:, :, :C_BG_SRC]
    head = jnp.pad(head, ((0, 0), (0, 0), (0, NG_PAD - N_NG)))
    return jnp.concatenate([head, w_in[:, :, C_BG_SRC:]], axis=2).astype(jnp.bfloat16)


def kernel(x_prompt, x_sample, cache_moba_kv, cache_diff_kv, cache_nsa_kv, state_nsa_win, state_ffn_conv, page_table, rel_bias, norm_attn, w_in, diff_lambda, diff_subln, nsa_cmp_pos, nsa_cmp_w1, nsa_cmp_b1, nsa_cmp_w2, nsa_cmp_b2, w_branch, w_out, norm_ffn, w_gate, w_up, w_down, conv_w, conv_b, norm_final):
    depth = w_in.shape[0]
    past_len = page_table.shape[1] * cache_moba_kv.shape[2]
    bf = jnp.bfloat16
    fpad = D_FF_PAD - D_FF
    w_in_p = _pack_w_in(w_in)
    w_branch_b = w_branch.astype(bf)
    w_out_b = w_out.astype(bf)
    w_gate_b = jnp.pad(w_gate, ((0, 0), (0, 0), (0, fpad))).astype(bf)
    w_up_b = jnp.pad(w_up, ((0, 0), (0, 0), (0, fpad))).astype(bf)
    w_down_b = jnp.pad(w_down, ((0, 0), (0, fpad), (0, 0))).astype(bf)
    conv_w_p = jnp.pad(conv_w, ((0, 0), (0, 0), (0, fpad)))
    conv_b_p = jnp.pad(conv_b, ((0, 0), (0, fpad))).reshape(depth, 1, D_FF_PAD)
    strip = rel_bias[jnp.asarray(_BUCKET)]
    bias = (strip, _bias_tiles(strip))
    caches = _page_views(cache_moba_kv, cache_diff_kv, cache_nsa_kv)
    dec_bias = _decode_bias(strip, x_sample.shape[1], cache_moba_kv.shape[2])

    y_p, y_s = x_prompt, x_sample
    outs_p = [[] for _ in range(5)]
    outs_s = [[] for _ in range(5)]
    for layer in range(depth):
        lam_init = 0.8 - 0.6 * math.exp(-0.3 * layer)
        wts = (norm_attn[layer], w_in_p[layer], diff_lambda[layer], diff_subln[layer], nsa_cmp_pos[layer],
               nsa_cmp_w1[layer], nsa_cmp_b1[layer], nsa_cmp_w2[layer], nsa_cmp_b2[layer], w_branch_b[layer],
               w_out_b[layer], norm_ffn[layer], w_gate_b[layer], w_up_b[layer], w_down_b[layer],
               conv_w_p[layer], conv_b_p[layer])
        y_p, *new_p = _layer(y_p, 0, None, None, None, bias, lam_init, wts)

        past = (layer, caches, page_table, dec_bias)
        y_s, *new_s = _layer(y_s, past_len, past, state_nsa_win[layer], state_ffn_conv[layer], bias, lam_init, wts)
        for lst, a in zip(outs_p, new_p):
            lst.append(a)
        for lst, a in zip(outs_s, new_s):
            lst.append(a)
    bp, tp, _ = y_p.shape
    bs, ts, _ = y_s.shape
    y_p = rmsnorm_rows(y_p.reshape(bp * tp, D_MODEL), norm_final, RMS_EPS, jnp.float32).reshape(bp, tp, D_MODEL)
    y_s = rmsnorm_rows(y_s.reshape(bs * ts, D_MODEL), norm_final, RMS_EPS, jnp.float32).reshape(bs, ts, D_MODEL)
    sp = [jnp.stack(l) for l in outs_p]
    ss = [jnp.stack(l) for l in outs_s]
    return (y_p, y_s, sp[0], ss[0], sp[1], ss[1], sp[2], ss[2], sp[3], ss[3], sp[4], ss[4])
```
